```python
import math
import jax, jax.numpy as jnp
from jax import lax
import numpy as np

D_MODEL = 2048
BATCH = 1
SEQ = 8192
DEPTH = 4

PE_DIM = 256
ATTN_HEADS = 8
ATTN_HEAD_DIM = 128
ATTN_WIDTH = ATTN_HEADS * ATTN_HEAD_DIM
MOBA_BLOCK = 256
MOBA_TOPK = 3
Q_CHUNK = 128
ROPE_DIM = ATTN_HEAD_DIM // 4
ROPE_THETA = 500000.0
SSD_WIDTH = D_MODEL - ATTN_WIDTH
SSD_HEAD_DIM = 64
SSD_HEADS = SSD_WIDTH // SSD_HEAD_DIM
SSD_GROUPS = 2
SSD_HEADS_PER_GROUP = SSD_HEADS // SSD_GROUPS
SSD_STATE = 128
SSD_CONV = 4
SSD_CHUNK = 256
SSD_CONV_DIM = SSD_WIDTH + 2 * SSD_GROUPS * SSD_STATE
MIX_WIDTH = ATTN_WIDTH + SSD_WIDTH
IN_PROJ_DIM = 3 * ATTN_WIDTH + SSD_WIDTH + SSD_CONV_DIM + SSD_HEADS
D_FF = 5632
FFN_CONV = 3
DEEPNORM_ALPHA = (2 * DEPTH) ** 0.25
DEEPNORM_BETA = (8 * DEPTH) ** -0.25
LN_EPS = 1e-5
RMS_EPS = 1e-6

kernel_name = "hybrid_moba_ssd_convffn_deepnorm"


def layer_norm(x, g, b):
    xf = x.astype(jnp.float32)
    mu = jnp.mean(xf, -1, keepdims=True)
    var = jnp.mean(jnp.square(xf - mu), -1, keepdims=True)
    return ((xf - mu) * lax.rsqrt(var + LN_EPS) * g + b).astype(x.dtype)


def rms_norm(x, g):
    xf = x.astype(jnp.float32)
    return (xf * lax.rsqrt(jnp.mean(xf * xf, -1, keepdims=True) + RMS_EPS) * g).astype(x.dtype)


def causal_depthwise_conv(x, w):
    K = w.shape[0]
    S = x.shape[1]
    xp = jnp.pad(x, ((0, 0), (K - 1, 0), (0, 0)))
    return sum(xp[:, k:k + S] * w[k] for k in range(K))


def partial_rope(t, positions):
    half = ROPE_DIM // 2
    inv_freq = 1.0 / (ROPE_THETA ** (jnp.arange(0, ROPE_DIM, 2, dtype=jnp.float32) / ROPE_DIM))
    ang = positions.astype(jnp.float32)[..., None] * inv_freq
    cos = jnp.cos(ang)[:, :, None, :]
    sin = jnp.sin(ang)[:, :, None, :]
    tf = t.astype(jnp.float32)
    t1, t2, rest = tf[..., :half], tf[..., half:ROPE_DIM], tf[..., ROPE_DIM:]
    out = jnp.concatenate([t1 * cos - t2 * sin, t2 * cos + t1 * sin, rest], -1)
    return out.astype(t.dtype)


def moba_attention(q, k, v):
    B, S, H, Dh = q.shape
    nb = -(-S // MOBA_BLOCK)
    sp = nb * MOBA_BLOCK
    topk = min(MOBA_TOPK, nb)
    scale = Dh ** -0.5
    qh = q.transpose(0, 2, 1, 3)
    pad = ((0, 0), (0, 0), (0, sp - S), (0, 0))
    kb = jnp.pad(k.transpose(0, 2, 1, 3), pad).reshape(B, H, nb, MOBA_BLOCK, Dh)
    vb = jnp.pad(v.transpose(0, 2, 1, 3), pad).reshape(B, H, nb, MOBA_BLOCK, Dh)
    k_mean = jnp.mean(kb.astype(jnp.float32), axis=3)
    q_blk = jnp.arange(S) // MOBA_BLOCK
    gate = jnp.einsum('bhsd,bhnd->bhsn', qh.astype(jnp.float32), k_mean)
    eligible = jnp.arange(nb)[None, :] < q_blk[:, None]
    gate = jnp.where(eligible, gate, -jnp.inf)
    _, sel = lax.top_k(gate, topk)
    valid = jnp.arange(topk)[None, :] < q_blk[:, None]
    bi = jnp.arange(B)[:, None, None, None]
    hi = jnp.arange(H)[None, :, None, None]
    n_chunks = S // Q_CHUNK
    sel_len = topk * MOBA_BLOCK

    def chunk(c):
        q0 = c * Q_CHUNK
        qc = lax.dynamic_slice_in_dim(qh, q0, Q_CHUNK, axis=2)
        sc = lax.dynamic_slice_in_dim(sel, q0, Q_CHUNK, axis=2)
        vc = lax.dynamic_slice_in_dim(valid, q0, Q_CHUNK, axis=0)
        own = q0 // MOBA_BLOCK
        k_own = lax.dynamic_index_in_dim(kb, own, axis=2, keepdims=False)
        v_own = lax.dynamic_index_in_dim(vb, own, axis=2, keepdims=False)
        k_sel = kb[bi, hi, sc]
        v_sel = vb[bi, hi, sc]
        s_sel = jnp.einsum('bhqd,bhqtkd->bhqtk', qc, k_sel).astype(jnp.float32) * scale
        s_sel = jnp.where(vc[None, None, :, :, None], s_sel, -jnp.inf)
        s_own = jnp.einsum('bhqd,bhkd->bhqk', qc, k_own).astype(jnp.float32) * scale
        q_pos = q0 + jnp.arange(Q_CHUNK)
        k_pos = own * MOBA_BLOCK + jnp.arange(MOBA_BLOCK)
        s_own = jnp.where(k_pos[None, :] <= q_pos[:, None], s_own, -jnp.inf)
        scores = jnp.concatenate([s_sel.reshape(B, H, Q_CHUNK, sel_len), s_own], -1)
        probs = jax.nn.softmax(scores, axis=-1).astype(v.dtype)
        p_sel = probs[..., :sel_len].reshape(B, H, Q_CHUNK, topk, MOBA_BLOCK)
        p_own = probs[..., sel_len:]
        return (jnp.einsum('bhqtk,bhqtkd->bhqd', p_sel, v_sel)
                + jnp.einsum('bhqk,bhkd->bhqd', p_own, v_own))

    out = lax.map(chunk, jnp.arange(n_chunks))
    return out.transpose(1, 0, 3, 2, 4).reshape(B, S, H * Dh)


def ssd_mixer(z, xbc, dt_raw, conv_w, conv_b, dt_bias, a_log, d_skip, norm_g):
    B, S, _ = xbc.shape
    G, E, P, N, L = SSD_GROUPS, SSD_HEADS_PER_GROUP, SSD_HEAD_DIM, SSD_STATE, SSD_CHUNK
    xbc = jax.nn.silu(causal_depthwise_conv(xbc, conv_w) + conv_b)
    xs, b_in, c_in = jnp.split(xbc, [SSD_WIDTH, SSD_WIDTH + G * N], axis=-1)
    dt = jax.nn.softplus(dt_raw.astype(jnp.float32) + dt_bias)
    a = -jnp.exp(a_log.astype(jnp.float32))
    nc = -(-S // L)
    sp = nc * L

    def pad_s(t):
        return jnp.pad(t, ((0, 0), (0, sp - S)) + ((0, 0),) * (t.ndim - 2))

    x5 = pad_s(xs).reshape(B, nc, L, G, E, P).astype(jnp.float32)
    dt5 = pad_s(dt).reshape(B, nc, L, G, E)
    bm = pad_s(b_in).reshape(B, nc, L, G, N).astype(jnp.float32)
    cm = pad_s(c_in).reshape(B, nc, L, G, N).astype(jnp.float32)
    xdt = x5 * dt5[..., None]
    a_cs = jnp.cumsum((dt5 * a.reshape(G, E)).transpose(0, 3, 4, 1, 2), axis=-1)
    tril = jnp.tril(jnp.ones((L, L), dtype=bool))
    seg = jnp.exp(jnp.where(tril, a_cs[..., :, None] - a_cs[..., None, :], -jnp.inf))
    cb = jnp.einsum('bclgn,bcsgn->bcgls', cm, bm)
    y_diag = jnp.einsum('bcgls,bgecls,bcsgep->bclgep', cb, seg, xdt)
    decay_states = jnp.exp(a_cs[..., -1:] - a_cs)
    states = jnp.einsum('bclgn,bgecl,bclgep->cbgepn', bm, decay_states, xdt)
    chunk_decay = jnp.exp(a_cs[..., -1]).transpose(3, 0, 1, 2)

    def step(h, inp):
        s_c, d_c = inp
        return d_c[..., None, None] * h + s_c, h

    _, states_in = lax.scan(step, jnp.zeros(states.shape[1:], jnp.float32), (states, chunk_decay))
    y_off = jnp.einsum('bclgn,cbgepn,bgecl->bclgep', cm, states_in, jnp.exp(a_cs))
    y = y_diag + y_off + x5 * d_skip.astype(jnp.float32).reshape(G, E)[:, :, None]
    y = y.reshape(B, sp, SSD_WIDTH)[:, :S]
    yg = (y * jax.nn.silu(z.astype(jnp.float32))).reshape(B, S, G, SSD_WIDTH // G)
    yg = yg * lax.rsqrt(jnp.mean(yg * yg, -1, keepdims=True) + RMS_EPS)
    return (yg.reshape(B, S, SSD_WIDTH) * norm_g).astype(z.dtype)


def conv_ffn(x, w_up, conv_w, conv_b, w_down):
    h = causal_depthwise_conv(x @ w_up, conv_w) + conv_b
    g, u = jnp.split(h, 2, axis=-1)
    return (jax.nn.silu(g) * u) @ w_down


def setup_inputs(seed: int = 0) -> dict:
    key = jax.random.key(seed)
    ks = jax.random.split(key, 24)
    nrm = jax.random.normal
    f32 = jnp.float32
    x = nrm(ks[0], (BATCH, SEQ, D_MODEL), f32)
    p = nrm(ks[1], (DEPTH, BATCH, SEQ, PE_DIM), f32)
    positions = jnp.broadcast_to(jnp.arange(SEQ, dtype=jnp.int32)[None, :], (BATCH, SEQ))
    w_in = nrm(ks[2], (DEPTH, D_MODEL, IN_PROJ_DIM), f32) * D_MODEL ** -0.5
    attn_norm_g = 1.0 + 0.02 * nrm(ks[3], (DEPTH, ATTN_WIDTH), f32)
    ssd_conv_w = nrm(ks[4], (DEPTH, SSD_CONV, SSD_CONV_DIM), f32) * SSD_CONV ** -0.5
    ssd_conv_b = 0.02 * nrm(ks[5], (DEPTH, SSD_CONV_DIM), f32)
    dt0 = jnp.exp(jax.random.uniform(ks[6], (DEPTH, SSD_HEADS), f32, math.log(1e-3), math.log(1e-1)))
    ssd_dt_bias = dt0 + jnp.log(-jnp.expm1(-dt0))
    ssd_a_log = jnp.log(jax.random.uniform(ks[7], (DEPTH, SSD_HEADS), f32, 1.0, 16.0))
    ssd_d = 1.0 + 0.1 * nrm(ks[8], (DEPTH, SSD_HEADS), f32)
    ssd_norm_g = 1.0 + 0.02 * nrm(ks[9], (DEPTH, SSD_WIDTH), f32)
    w_out = nrm(ks[10], (DEPTH, MIX_WIDTH, D_MODEL), f32) * MIX_WIDTH ** -0.5 * DEEPNORM_BETA
    ln1_g = 1.0 + 0.02 * nrm(ks[11], (DEPTH, D_MODEL), f32)
    ln1_b = 0.02 * nrm(ks[12], (DEPTH, D_MODEL), f32)
    w_up = nrm(ks[13], (DEPTH, D_MODEL, 2 * D_FF), f32) * D_MODEL ** -0.5
    ffn_conv_w = nrm(ks[14], (DEPTH, FFN_CONV, 2 * D_FF), f32) * FFN_CONV ** -0.5
    ffn_conv_b = 0.02 * nrm(ks[15], (DEPTH, 2 * D_FF), f32)
    w_down = nrm(ks[16], (DEPTH, D_FF, D_MODEL), f32) * D_FF ** -0.5 * DEEPNORM_BETA
    w_pe_gate = nrm(ks[17], (DEPTH, D_MODEL, D_MODEL), f32) * D_MODEL ** -0.5
    w_pe_proj = nrm(ks[18], (DEPTH, PE_DIM, D_MODEL), f32) * PE_DIM ** -0.5 * DEEPNORM_BETA
    ln2_g = 1.0 + 0.02 * nrm(ks[19], (DEPTH, D_MODEL), f32)
    ln2_b = 0.02 * nrm(ks[20], (DEPTH, D_MODEL), f32)
    return {"x": x, "p": p, "positions": positions, "w_in": w_in, "attn_norm_g": attn_norm_g,
            "ssd_conv_w": ssd_conv_w, "ssd_conv_b": ssd_conv_b, "ssd_dt_bias": ssd_dt_bias,
            "ssd_a_log": ssd_a_log, "ssd_d": ssd_d, "ssd_norm_g": ssd_norm_g, "w_out": w_out,
            "ln1_g": ln1_g, "ln1_b": ln1_b, "w_up": w_up, "ffn_conv_w": ffn_conv_w,
            "ffn_conv_b": ffn_conv_b, "w_down": w_down, "w_pe_gate": w_pe_gate,
            "w_pe_proj": w_pe_proj, "ln2_g": ln2_g, "ln2_b": ln2_b}


def reference(x, p, positions, w_in, attn_norm_g, ssd_conv_w, ssd_conv_b, ssd_dt_bias, ssd_a_log,
              ssd_d, ssd_norm_g, w_out, ln1_g, ln1_b, w_up, ffn_conv_w, ffn_conv_b, w_down,
              w_pe_gate, w_pe_proj, ln2_g, ln2_b):
    B, S, _ = x.shape
    A = ATTN_WIDTH
    splits = [A, 2 * A, 3 * A, 3 * A + SSD_WIDTH, 3 * A + SSD_WIDTH + SSD_CONV_DIM]
    h = x
    for i in range(DEPTH):
        proj = h @ w_in[i]
        q, k, v, z, xbc, dt_raw = jnp.split(proj, splits, axis=-1)
        q = partial_rope(q.reshape(B, S, ATTN_HEADS, ATTN_HEAD_DIM), positions)
        k = partial_rope(k.reshape(B, S, ATTN_HEADS, ATTN_HEAD_DIM), positions)
        v = v.reshape(B, S, ATTN_HEADS, ATTN_HEAD_DIM)
        attn_out = rms_norm(moba_attention(q, k, v), attn_norm_g[i])
        ssd_out = ssd_mixer(z, xbc, dt_raw, ssd_conv_w[i], ssd_conv_b[i], ssd_dt_bias[i],
                            ssd_a_log[i], ssd_d[i], ssd_norm_g[i])
        mix = jnp.concatenate([attn_out, ssd_out], axis=-1) @ w_out[i]
        h = layer_norm(DEEPNORM_ALPHA * h + mix, ln1_g[i], ln1_b[i])
        ffn = conv_ffn(h, w_up[i], ffn_conv_w[i], ffn_conv_b[i], w_down[i])
        ple = (p[i] @ w_pe_proj[i]) * jax.nn.sigmoid(h @ w_pe_gate[i])
        h = layer_norm(DEEPNORM_ALPHA * h + ffn + ple, ln2_g[i], ln2_b[i])
    return h
```

```python
import functools

import jax
import jax.numpy as jnp
from jax import lax
from jax.experimental import pallas as pl
from jax.experimental.pallas import tpu as pltpu

F32 = jnp.float32
BF16 = jnp.bfloat16

D_MODEL = 2048
DEPTH = 4
PE_DIM = 256
ATTN_HEADS = 8
ATTN_HEAD_DIM = 128
ATTN_WIDTH = ATTN_HEADS * ATTN_HEAD_DIM
MOBA_BLOCK = 256
MOBA_TOPK = 3
ROPE_DIM = ATTN_HEAD_DIM // 4
ROPE_THETA = 500000.0
SSD_WIDTH = D_MODEL - ATTN_WIDTH
SSD_HEAD_DIM = 64
SSD_HEADS = SSD_WIDTH // SSD_HEAD_DIM
SSD_GROUPS = 2
SSD_STATE = 128
SSD_CONV = 4
SSD_CHUNK = 256
SSD_BC = 2 * SSD_GROUPS * SSD_STATE
SSD_CONV_DIM = SSD_WIDTH + SSD_BC
D_FF = 5632
FFN_CONV = 3
DEEPNORM_ALPHA = (2 * DEPTH) ** 0.25
LN_EPS = 1e-5
RMS_EPS = 1e-6

LANES = 128
SUBLANES = 8
VMEM_LIMIT = 56 * 1024 * 1024

NEG_BIG = -1e30
DT_PAD = LANES
ZX_WIDTH = SSD_WIDTH + SSD_CONV_DIM + DT_PAD


def _params(*sem):
    return pltpu.CompilerParams(dimension_semantics=sem, vmem_limit_bytes=VMEM_LIMIT)


def _tile(n, pref):
    t = min(n, pref)
    assert n % t == 0, (n, t)
    return t


def _rope_table_kernel(pos_ref, invf_ref, sgn_ref, cos_ref, sin_ref):
    ang = pos_ref[...].astype(F32) * invf_ref[...]
    cos_ref[...] = jnp.cos(ang)
    sin_ref[...] = jnp.sin(ang) * sgn_ref[...]


def _rope_tables(positions):
    s = positions.shape[0]
    half = ROPE_DIM // 2
    inv_freq = 1.0 / (ROPE_THETA ** (jnp.arange(0, ROPE_DIM, 2, dtype=F32) / ROPE_DIM))
    zeros = jnp.zeros((LANES - ROPE_DIM,), F32)
    invf = jnp.concatenate([inv_freq, inv_freq, zeros]).reshape(1, LANES)
    sgn = jnp.concatenate([-jnp.ones((half,), F32), jnp.ones((half,), F32), zeros]).reshape(1, LANES)
    tm = _tile(s, 1024)
    return pl.pallas_call(
        _rope_table_kernel,
        out_shape=(jax.ShapeDtypeStruct((s, LANES), F32), jax.ShapeDtypeStruct((s, LANES), F32)),
        grid=(s // tm,),
        in_specs=[pl.BlockSpec((tm, 1), lambda i: (i, 0)),
                  pl.BlockSpec((1, LANES), lambda i: (0, 0)),
                  pl.BlockSpec((1, LANES), lambda i: (0, 0))],
        out_specs=(pl.BlockSpec((tm, LANES), lambda i: (i, 0)),
                   pl.BlockSpec((tm, LANES), lambda i: (i, 0))),
        compiler_params=_params("parallel"),
        name="rope_tables",
    )(positions.reshape(s, 1), invf, sgn)


def _qkv_kernel(x_ref, w_ref, cos_ref, sin_ref, o_ref, *, rope_tiles):
    acc = jnp.dot(x_ref[...], w_ref[...], preferred_element_type=F32)
    j = pl.program_id(1)

    @pl.when(j < rope_tiles)
    def _():
        cos = cos_ref[...]
        sin = sin_ref[...]
        lane = lax.broadcasted_iota(jnp.int32, cos.shape, 1)
        half = ROPE_DIM // 2
        for c in range(acc.shape[1] // LANES):
            t = acc[:, c * LANES:(c + 1) * LANES]
            partner = jnp.where(lane < half, pltpu.roll(t, LANES - half, 1), pltpu.roll(t, half, 1))
            o_ref[:, c * LANES:(c + 1) * LANES] = (t * cos + partner * sin).astype(o_ref.dtype)

    @pl.when(j >= rope_tiles)
    def _():
        o_ref[...] = acc.astype(o_ref.dtype)


def _qkv_proj(xb, w, cos, sin):
    s, k = xb.shape
    n = w.shape[1]
    tm, tn = _tile(s, 1024), 512
    return pl.pallas_call(
        functools.partial(_qkv_kernel, rope_tiles=2 * ATTN_WIDTH // tn),
        out_shape=jax.ShapeDtypeStruct((s, n), BF16),
        grid=(s // tm, n // tn),
        in_specs=[pl.BlockSpec((tm, k), lambda i, j: (i, 0)),
                  pl.BlockSpec((k, tn), lambda i, j: (0, j)),
                  pl.BlockSpec((tm, LANES), lambda i, j: (i, 0)),
                  pl.BlockSpec((tm, LANES), lambda i, j: (i, 0))],
        out_specs=pl.BlockSpec((tm, tn), lambda i, j: (i, j)),
        compiler_params=_params("parallel", "arbitrary"),
        name="qkv_proj",
    )(xb, w, cos, sin)


def _mm_kernel(x_ref, w_ref, o_ref):
    o_ref[...] = jnp.dot(x_ref[...], w_ref[...], preferred_element_type=F32).astype(o_ref.dtype)


def _matmul(xb, w, out_dtype, tn):
    s, k = xb.shape
    n = w.shape[1]
    tm = _tile(s, 1024)
    return pl.pallas_call(
        _mm_kernel,
        out_shape=jax.ShapeDtypeStruct((s, n), out_dtype),
        grid=(s // tm, n // tn),
        in_specs=[pl.BlockSpec((tm, k), lambda i, j: (i, 0)),
                  pl.BlockSpec((k, tn), lambda i, j: (0, j))],
        out_specs=pl.BlockSpec((tm, tn), lambda i, j: (i, j)),
        compiler_params=_params("parallel", "arbitrary"),
        name="zx_proj",
    )(xb, w)


def _moba_kernel(q_ref, k_ref, v_ref, o_ref, kmean_ref, sel_ref, *, nb):
    bs = MOBA_BLOCK
    qi = pl.program_id(1)
    scale = ATTN_HEAD_DIM ** -0.5

    @pl.when(qi == 0)
    def _():
        kmean_ref[...] = jnp.zeros_like(kmean_ref)

        def body(n, carry):
            kb = k_ref[pl.ds(pl.multiple_of(n * bs, bs), bs), :].astype(F32)
            kmean_ref[pl.ds(n, 1), :] = jnp.mean(kb, axis=0, keepdims=True)
            return carry

        lax.fori_loop(0, nb, body, 0)

    q = q_ref[...]
    nt = (((1,), (1,)), ((), ()))

    gate = lax.dot_general(q, kmean_ref[...].astype(BF16), nt, preferred_element_type=F32)
    lane = lax.broadcasted_iota(jnp.int32, gate.shape, 1)
    g = jnp.where(lane < qi, gate, -jnp.inf)
    sel = jnp.zeros(gate.shape, F32)
    for _ in range(MOBA_TOPK):
        mx = jnp.max(g, axis=-1, keepdims=True)
        first = jnp.min(jnp.where(g == mx, lane, LANES), axis=-1, keepdims=True)
        pick = lane == jnp.where(mx > -jnp.inf, first, -1)
        sel = jnp.where(pick, 1.0, sel)
        g = jnp.where(pick, -jnp.inf, g)
    sel_ref[...] = sel

    own = pl.multiple_of(qi * bs, bs)
    kb = k_ref[pl.ds(own, bs), :]
    vb = v_ref[pl.ds(own, bs), :]
    s = lax.dot_general(q, kb, nt, preferred_element_type=F32) * scale
    row = lax.broadcasted_iota(jnp.int32, s.shape, 0)
    col = lax.broadcasted_iota(jnp.int32, s.shape, 1)
    s = jnp.where(col <= row, s, NEG_BIG)
    m0 = jnp.max(s, axis=-1, keepdims=True)
    p = jnp.exp(s - m0)
    l0 = jnp.sum(p, axis=-1, keepdims=True)
    acc0 = jnp.dot(p.astype(BF16), vb, preferred_element_type=F32)

    def body(j, carry):
        m, l, acc = carry
        start = pl.multiple_of(j * bs, bs)
        kb = k_ref[pl.ds(start, bs), :]
        vb = v_ref[pl.ds(start, bs), :]
        picked = jnp.sum(jnp.where(lane == j, sel_ref[...], 0.0), axis=-1, keepdims=True) > 0.0
        s = lax.dot_general(q, kb, nt, preferred_element_type=F32) * scale
        s = jnp.where(picked, s, NEG_BIG)
        m_new = jnp.maximum(m, jnp.max(s, axis=-1, keepdims=True))
        alpha = jnp.exp(m - m_new)
        p = jnp.exp(s - m_new)
        l = alpha * l + jnp.sum(p, axis=-1, keepdims=True)
        acc = alpha * acc + jnp.dot(p.astype(BF16), vb, preferred_element_type=F32)
        return m_new, l, acc

    m, l, acc = lax.fori_loop(0, qi, body, (m0, l0, acc0))
    o_ref[...] = (acc / l).astype(o_ref.dtype)


def _moba_attention(qkv):
    s = qkv.shape[0]
    bs = MOBA_BLOCK
    assert s % bs == 0 and s // bs <= LANES
    nb = s // bs
    h_blocks = ATTN_HEADS
    return pl.pallas_call(
        functools.partial(_moba_kernel, nb=nb),
        out_shape=jax.ShapeDtypeStruct((s, ATTN_WIDTH), F32),
        grid=(ATTN_HEADS, nb),
        in_specs=[pl.BlockSpec((bs, ATTN_HEAD_DIM), lambda h, i: (i, h)),
                  pl.BlockSpec((s, ATTN_HEAD_DIM), lambda h, i: (0, h_blocks + h)),
                  pl.BlockSpec((s, ATTN_HEAD_DIM), lambda h, i: (0, 2 * h_blocks + h))],
        out_specs=pl.BlockSpec((bs, ATTN_HEAD_DIM), lambda h, i: (i, h)),
        scratch_shapes=[pltpu.VMEM((LANES, ATTN_HEAD_DIM), F32),
                        pltpu.VMEM((bs, LANES), F32)],
        compiler_params=_params("parallel", "arbitrary"),
        name="moba_attention",
    )(qkv, qkv, qkv)


def _silu(x):
    return x * jax.nn.sigmoid(x)


def _ssd_kernel(z_ref, xs_ref, bc_ref, dt_ref, cwx_ref, cwbc_ref, cbx_ref, cbbc_ref, dtb_ref, alog_ref,
                dskip_ref, ng_ref, expand_ref, o_ref, xpad_ref, bcpad_ref, state_ref, y_ref):
    L = SSD_CHUNK
    P = SSD_HEAD_DIM
    N = SSD_STATE
    halo = SUBLANES
    c = pl.program_id(0)
    hi = lax.Precision.HIGHEST

    @pl.when(c == 0)
    def _():
        xpad_ref[0:halo, :] = jnp.zeros((halo, xpad_ref.shape[1]), F32)
        bcpad_ref[0:halo, :] = jnp.zeros((halo, bcpad_ref.shape[1]), F32)
        state_ref[...] = jnp.zeros_like(state_ref)

    xpad_ref[halo:halo + L, :] = xs_ref[...]
    bcpad_ref[halo:halo + L, :] = bc_ref[...]

    def conv(pad_ref, w_ref, b_ref):
        acc = b_ref[...]
        for k in range(SSD_CONV):
            off = halo - (SSD_CONV - 1) + k
            acc = acc + pad_ref[off:off + L, :] * w_ref[k:k + 1, :]
        return _silu(acc)

    xs = conv(xpad_ref, cwx_ref, cbx_ref)
    bc = conv(bcpad_ref, cwbc_ref, cbbc_ref)
    xpad_ref[0:halo, :] = xpad_ref[L:L + halo, :]
    bcpad_ref[0:halo, :] = bcpad_ref[L:L + halo, :]

    lane = lax.broadcasted_iota(jnp.int32, (1, LANES), 1)
    dtr = dt_ref[...] + dtb_ref[...]
    dt = jnp.maximum(dtr, 0.0) + jnp.log1p(jnp.exp(-jnp.abs(dtr)))
    a = jnp.where(lane < SSD_HEADS, -jnp.exp(alog_ref[...]), 0.0)
    da = dt * a
    row = lax.broadcasted_iota(jnp.int32, (L, L), 0)
    col = lax.broadcasted_iota(jnp.int32, (L, L), 1)
    tril = row >= col
    a_cs = jnp.dot(tril.astype(F32), da, preferred_element_type=F32, precision=hi)
    a_cs_t = a_cs.T
    exp_a = jnp.exp(a_cs)
    decay = jnp.exp(a_cs[L - 1:L, :] - a_cs)
    stacked = jnp.concatenate([dt, exp_a, decay], axis=0)
    wide = jnp.dot(stacked, expand_ref[...], preferred_element_type=F32, precision=hi)
    dt_w, exp_w, dec_w = wide[0:L], wide[L:2 * L], wide[2 * L:3 * L]
    xdt = xs * dt_w
    xdec_b = (xdt * dec_w).astype(BF16)
    lane_w = lax.broadcasted_iota(jnp.int32, (L, LANES), 1)
    first_head = lane_w < P

    gw = SSD_WIDTH // SSD_GROUPS
    pairs_per_group = gw // LANES
    for g in range(SSD_GROUPS):
        b_g = bc[:, g * N:(g + 1) * N]
        c_g = bc[:, SSD_GROUPS * N + g * N:SSD_GROUPS * N + (g + 1) * N].astype(BF16)
        b_t = b_g.T.astype(BF16)
        cb = jnp.dot(c_g, b_t, preferred_element_type=F32)
        ssq = jnp.zeros((L, 1), F32)
        for i in range(pairs_per_group):
            pi = g * pairs_per_group + i
            blk = slice(pi * LANES, (pi + 1) * LANES)
            x_blk = xdt[:, blk]
            y_pair = jnp.zeros((L, LANES), F32)
            for e in range(2):
                h = 2 * pi + e
                seg = jnp.exp(jnp.where(tril, a_cs[:, h:h + 1] - a_cs_t[h:h + 1, :], -jnp.inf))
                mat = (cb * seg).astype(BF16)
                keep = first_head if e == 0 else jnp.logical_not(first_head)
                y_pair = y_pair + jnp.dot(mat, jnp.where(keep, x_blk, 0.0).astype(BF16),
                                          preferred_element_type=F32)
            st = jnp.dot(b_t, xdec_b[:, blk], preferred_element_type=F32)
            h_prev = state_ref[pi]
            y_off = jnp.dot(c_g, h_prev.astype(BF16), preferred_element_type=F32) * exp_w[:, blk]
            state_ref[pi] = h_prev * exp_w[L - 1:L, blk] + st
            y = y_pair + y_off + xs[:, blk] * dskip_ref[:, blk]
            yg = y * _silu(z_ref[:, blk])
            ssq = ssq + jnp.sum(yg * yg, axis=-1, keepdims=True)
            y_ref[:, blk] = yg
        rs = lax.rsqrt(ssq / gw + RMS_EPS)
        gsl = slice(g * gw, (g + 1) * gw)
        o_ref[:, gsl] = (y_ref[:, gsl] * rs * ng_ref[:, gsl]).astype(o_ref.dtype)


def _pad_lanes(v, width=LANES):
    return jnp.pad(v.astype(F32), (0, width - v.shape[0])).reshape(1, width)


def _ssd(zx, conv_w, conv_b, dt_bias, a_log, d_skip, norm_g):
    s = zx.shape[0]
    L = SSD_CHUNK
    assert s % L == 0
    w = SSD_WIDTH
    expand = (jnp.arange(LANES)[:, None] == (jnp.arange(w)[None, :] // SSD_HEAD_DIM)).astype(F32)
    d_wide = jnp.repeat(d_skip.astype(F32), SSD_HEAD_DIM).reshape(1, w)
    row1 = lambda width: pl.BlockSpec((1, width), lambda c: (0, 0))
    return pl.pallas_call(
        _ssd_kernel,
        out_shape=jax.ShapeDtypeStruct((s, w), BF16),
        grid=(s // L,),
        in_specs=[pl.BlockSpec((L, w), lambda c: (c, 0)),
                  pl.BlockSpec((L, w), lambda c: (c, 1)),
                  pl.BlockSpec((L, SSD_BC), lambda c: (c, 2 * w // SSD_BC)),
                  pl.BlockSpec((L, DT_PAD), lambda c: (c, (2 * w + SSD_BC) // DT_PAD)),
                  pl.BlockSpec((SSD_CONV, w), lambda c: (0, 0)),
                  pl.BlockSpec((SSD_CONV, SSD_BC), lambda c: (0, 0)),
                  row1(w), row1(SSD_BC), row1(LANES), row1(LANES), row1(w), row1(w),
                  pl.BlockSpec((LANES, w), lambda c: (0, 0))],
        out_specs=pl.BlockSpec((L, w), lambda c: (c, 0)),
        scratch_shapes=[pltpu.VMEM((L + SUBLANES, w), F32),
                        pltpu.VMEM((L + SUBLANES, SSD_BC), F32),
                        pltpu.VMEM((w // LANES, SSD_STATE, LANES), F32),
                        pltpu.VMEM((L, w), F32)],
        compiler_params=_params("arbitrary"),
        name="ssd_scan",
    )(zx, zx, zx, zx, conv_w[:, :w], conv_w[:, w:], conv_b[:w].reshape(1, w), conv_b[w:].reshape(1, SSD_BC),
      _pad_lanes(dt_bias), _pad_lanes(a_log), d_wide, norm_g.reshape(1, w), expand)


def _layer_norm(y, g, b):
    mu = jnp.mean(y, axis=-1, keepdims=True)
    d = y - mu
    var = jnp.mean(d * d, axis=-1, keepdims=True)
    return d * lax.rsqrt(var + LN_EPS) * g + b


def _outproj_kernel(attn_ref, ssd_ref, h_ref, w_ref, ag_ref, g_ref, b_ref, o_ref, ob_ref):
    attn = attn_ref[...]
    attn_n = attn * lax.rsqrt(jnp.mean(attn * attn, axis=-1, keepdims=True) + RMS_EPS) * ag_ref[...]
    aw = ATTN_WIDTH
    mix = jnp.dot(attn_n.astype(BF16), w_ref[0:aw, :], preferred_element_type=F32)
    mix = mix + jnp.dot(ssd_ref[...], w_ref[aw:, :], preferred_element_type=F32)
    out = _layer_norm(DEEPNORM_ALPHA * h_ref[...] + mix, g_ref[...], b_ref[...])
    o_ref[...] = out
    ob_ref[...] = out.astype(BF16)


def _outproj_ln(attn, ssd, h, w, attn_g, ln_g, ln_b):
    s, d = h.shape
    tm = _tile(s, 512)
    row = lambda width: pl.BlockSpec((1, width), lambda i: (0, 0))
    return pl.pallas_call(
        _outproj_kernel,
        out_shape=(jax.ShapeDtypeStruct((s, d), F32), jax.ShapeDtypeStruct((s, d), BF16)),
        grid=(s // tm,),
        in_specs=[pl.BlockSpec((tm, ATTN_WIDTH), lambda i: (i, 0)),
                  pl.BlockSpec((tm, SSD_WIDTH), lambda i: (i, 0)),
                  pl.BlockSpec((tm, d), lambda i: (i, 0)),
                  pl.BlockSpec((d, d), lambda i: (0, 0)),
                  row(ATTN_WIDTH), row(d), row(d)],
        out_specs=(pl.BlockSpec((tm, d), lambda i: (i, 0)), pl.BlockSpec((tm, d), lambda i: (i, 0))),
        compiler_params=_params("parallel"),
        name="outproj_ln",
    )(attn, ssd, h, w, attn_g.reshape(1, -1), ln_g.reshape(1, -1), ln_b.reshape(1, -1))


def _ffn_up_kernel(x_ref, wg_ref, wu_ref, cwg_ref, cwu_ref, cbg_ref, cbu_ref, o_ref, gpad_ref, upad_ref):
    tm = x_ref.shape[0]
    halo = SUBLANES
    i = pl.program_id(1)

    @pl.when(i == 0)
    def _():
        gpad_ref[0:halo, :] = jnp.zeros((halo, gpad_ref.shape[1]), F32)
        upad_ref[0:halo, :] = jnp.zeros((halo, upad_ref.shape[1]), F32)

    x = x_ref[...]
    gpad_ref[halo:halo + tm, :] = jnp.dot(x, wg_ref[...], preferred_element_type=F32)
    upad_ref[halo:halo + tm, :] = jnp.dot(x, wu_ref[...], preferred_element_type=F32)

    def conv(pad_ref, w_ref, b_ref):
        acc = b_ref[...]
        for k in range(FFN_CONV):
            off = halo - (FFN_CONV - 1) + k
            acc = acc + pad_ref[off:off + tm, :] * w_ref[k:k + 1, :]
        return acc

    g = conv(gpad_ref, cwg_ref, cbg_ref)
    u = conv(upad_ref, cwu_ref, cbu_ref)
    o_ref[...] = (_silu(g) * u).astype(o_ref.dtype)
    gpad_ref[0:halo, :] = gpad_ref[tm:tm + halo, :]
    upad_ref[0:halo, :] = upad_ref[tm:tm + halo, :]


def _ffn_up(hb, w_up, conv_w, conv_b):
    s, d = hb.shape
    tm, tn = _tile(s, 1024), 512
    nj = D_FF // tn
    conv_b = conv_b.reshape(1, -1)
    return pl.pallas_call(
        _ffn_up_kernel,
        out_shape=jax.ShapeDtypeStruct((s, D_FF), BF16),
        grid=(nj, s // tm),
        in_specs=[pl.BlockSpec((tm, d), lambda j, i: (i, 0)),
                  pl.BlockSpec((d, tn), lambda j, i: (0, j)),
                  pl.BlockSpec((d, tn), lambda j, i: (0, j + nj)),
                  pl.BlockSpec((FFN_CONV, tn), lambda j, i: (0, j)),
                  pl.BlockSpec((FFN_CONV, tn), lambda j, i: (0, j + nj)),
                  pl.BlockSpec((1, tn), lambda j, i: (0, j)),
                  pl.BlockSpec((1, tn), lambda j, i: (0, j + nj))],
        out_specs=pl.BlockSpec((tm, tn), lambda j, i: (i, j)),
        scratch_shapes=[pltpu.VMEM((tm + SUBLANES, tn), F32), pltpu.VMEM((tm + SUBLANES, tn), F32)],
        compiler_params=_params("parallel", "arbitrary"),
        name="ffn_up",
    )(hb, w_up, w_up, conv_w, conv_w, conv_b, conv_b)


def _ple_kernel(p_ref, hb_ref, wp_ref, wg_ref, o_ref):
    proj = jnp.dot(p_ref[...].astype(BF16), wp_ref[...], preferred_element_type=F32)
    gate = jnp.dot(hb_ref[...], wg_ref[...], preferred_element_type=F32)
    o_ref[...] = proj * jax.nn.sigmoid(gate)


def _ple(p, hb, w_proj, w_gate):
    s, d = hb.shape
    tm, tn = _tile(s, 1024), 1024
    return pl.pallas_call(
        _ple_kernel,
        out_shape=jax.ShapeDtypeStruct((s, d), F32),
        grid=(s // tm, d // tn),
        in_specs=[pl.BlockSpec((tm, PE_DIM), lambda i, j: (i, 0)),
                  pl.BlockSpec((tm, d), lambda i, j: (i, 0)),
                  pl.BlockSpec((PE_DIM, tn), lambda i, j: (0, j)),
                  pl.BlockSpec((d, tn), lambda i, j: (0, j))],
        out_specs=pl.BlockSpec((tm, tn), lambda i, j: (i, j)),
        compiler_params=_params("parallel", "arbitrary"),
        name="ple_gate",
    )(p, hb, w_proj, w_gate)


def _down_kernel(a_ref, w_ref, h_ref, ple_ref, g_ref, b_ref, o_ref, ob_ref, acc_ref):
    k = pl.program_id(1)

    @pl.when(k == 0)
    def _():
        acc_ref[...] = jnp.zeros_like(acc_ref)

    acc_ref[...] += jnp.dot(a_ref[...], w_ref[...], preferred_element_type=F32)

    @pl.when(k == pl.num_programs(1) - 1)
    def _():
        out = _layer_norm(DEEPNORM_ALPHA * h_ref[...] + acc_ref[...] + ple_ref[...], g_ref[...], b_ref[...])
        o_ref[...] = out
        ob_ref[...] = out.astype(BF16)


def _down_ln(act, w_down, h, ple, ln_g, ln_b):
    s, d = h.shape
    tm, tk = _tile(s, 512), D_FF // 4
    row = lambda width: pl.BlockSpec((1, width), lambda i, k: (0, 0))
    return pl.pallas_call(
        _down_kernel,
        out_shape=(jax.ShapeDtypeStruct((s, d), F32), jax.ShapeDtypeStruct((s, d), BF16)),
        grid=(s // tm, D_FF // tk),
        in_specs=[pl.BlockSpec((tm, tk), lambda i, k: (i, k)),
                  pl.BlockSpec((tk, d), lambda i, k: (k, 0)),
                  pl.BlockSpec((tm, d), lambda i, k: (i, 0)),
                  pl.BlockSpec((tm, d), lambda i, k: (i, 0)),
                  row(d), row(d)],
        out_specs=(pl.BlockSpec((tm, d), lambda i, k: (i, 0)), pl.BlockSpec((tm, d), lambda i, k: (i, 0))),
        scratch_shapes=[pltpu.VMEM((tm, d), F32)],
        compiler_params=_params("parallel", "arbitrary"),
        name="down_ln",
    )(act, w_down, h, ple, ln_g.reshape(1, -1), ln_b.reshape(1, -1))


def _layer(h, hb, p_i, cos, sin, w_in, attn_norm_g, ssd_conv_w, ssd_conv_b, ssd_dt_bias, ssd_a_log, ssd_d,
           ssd_norm_g, w_out, ln1_g, ln1_b, w_up, ffn_conv_w, ffn_conv_b, w_down, w_pe_gate, w_pe_proj,
           ln2_g, ln2_b):
    qkv_w = 3 * ATTN_WIDTH
    w_qkv = w_in[:, :qkv_w].astype(BF16)
    n_dt = w_in.shape[1] - qkv_w - SSD_WIDTH - SSD_CONV_DIM
    w_zx = jnp.pad(w_in[:, qkv_w:], ((0, 0), (0, DT_PAD - n_dt))).astype(BF16)
    qkv = _qkv_proj(hb, w_qkv, cos, sin)
    zx = _matmul(hb, w_zx, F32, ZX_WIDTH // 3)
    attn = _moba_attention(qkv)
    ssd = _ssd(zx, ssd_conv_w, ssd_conv_b, ssd_dt_bias, ssd_a_log, ssd_d, ssd_norm_g)
    h1, h1b = _outproj_ln(attn, ssd, h, w_out.astype(BF16), attn_norm_g, ln1_g, ln1_b)
    act = _ffn_up(h1b, w_up.astype(BF16), ffn_conv_w, ffn_conv_b)
    ple = _ple(p_i, h1b, w_pe_proj.astype(BF16), w_pe_gate.astype(BF16))
    return _down_ln(act, w_down.astype(BF16), h1, ple, ln2_g, ln2_b)


def kernel(x, p, positions, w_in, attn_norm_g, ssd_conv_w, ssd_conv_b, ssd_dt_bias, ssd_a_log, ssd_d, ssd_norm_g,
           w_out, ln1_g, ln1_b, w_up, ffn_conv_w, ffn_conv_b, w_down, w_pe_gate, w_pe_proj, ln2_g, ln2_b):
    b, s, d = x.shape
    assert b == 1 and d == D_MODEL
    cos, sin = _rope_tables(positions[0])
    h = x[0]
    hb = h.astype(BF16)
    for i in range(w_in.shape[0]):
        h, hb = _layer(h, hb, p[i, 0], cos, sin, w_in[i], attn_norm_g[i], ssd_conv_w[i], ssd_conv_b[i],
                       ssd_dt_bias[i], ssd_a_log[i], ssd_d[i], ssd_norm_g[i], w_out[i], ln1_g[i], ln1_b[i],
                       w_up[i], ffn_conv_w[i], ffn_conv_b[i], w_down[i], w_pe_gate[i], w_pe_proj[i],
                       ln2_g[i], ln2_b[i])
    return h[None]
```

```python
import functools

import jax
import jax.numpy as jnp
from jax import lax
from jax.experimental import pallas as pl
from jax.experimental.pallas import tpu as pltpu

F32 = jnp.float32
BF16 = jnp.bfloat16

D_MODEL = 2048
DEPTH = 4
PE_DIM = 256
ATTN_HEADS = 8
ATTN_HEAD_DIM = 128
ATTN_WIDTH = ATTN_HEADS * ATTN_HEAD_DIM
MOBA_BLOCK = 256
MOBA_TOPK = 3
ROPE_DIM = ATTN_HEAD_DIM // 4
ROPE_THETA = 500000.0
SSD_WIDTH = D_MODEL - ATTN_WIDTH
SSD_HEAD_DIM = 64
SSD_HEADS = SSD_WIDTH // SSD_HEAD_DIM
SSD_GROUPS = 2
SSD_STATE = 128
SSD_CONV = 4
SSD_CHUNK = 256
SSD_BC = 2 * SSD_GROUPS * SSD_STATE
SSD_CONV_DIM = SSD_WIDTH + SSD_BC
D_FF = 5632
FFN_CONV = 3
DEEPNORM_ALPHA = (2 * DEPTH) ** 0.25
LN_EPS = 1e-5
RMS_EPS = 1e-6

LANES = 128
SUBLANES = 8
VMEM_LIMIT = 56 * 1024 * 1024

NEG_BIG = -1e30
DT_PAD = LANES
ZX_WIDTH = SSD_WIDTH + SSD_CONV_DIM + DT_PAD


def _params(*sem):
    return pltpu.CompilerParams(dimension_semantics=sem, vmem_limit_bytes=VMEM_LIMIT)


def _tile(n, pref):
    t = min(n, pref)
    assert n % t == 0, (n, t)
    return t


def _rope_table_kernel(pos_ref, invf_ref, sgn_ref, cos_ref, sin_ref):
    ang = pos_ref[...].astype(F32) * invf_ref[...]
    cos_ref[...] = jnp.cos(ang)
    sin_ref[...] = jnp.sin(ang) * sgn_ref[...]


def _rope_tables(positions):
    s = positions.shape[0]
    half = ROPE_DIM // 2
    inv_freq = 1.0 / (ROPE_THETA ** (jnp.arange(0, ROPE_DIM, 2, dtype=F32) / ROPE_DIM))
    zeros = jnp.zeros((LANES - ROPE_DIM,), F32)
    invf = jnp.concatenate([inv_freq, inv_freq, zeros]).reshape(1, LANES)
    sgn = jnp.concatenate([-jnp.ones((half,), F32), jnp.ones((half,), F32), zeros]).reshape(1, LANES)
    tm = _tile(s, 1024)
    return pl.pallas_call(
        _rope_table_kernel,
        out_shape=(jax.ShapeDtypeStruct((s, LANES), F32), jax.ShapeDtypeStruct((s, LANES), F32)),
        grid=(s // tm,),
        in_specs=[pl.BlockSpec((tm, 1), lambda i: (i, 0)),
                  pl.BlockSpec((1, LANES), lambda i: (0, 0)),
                  pl.BlockSpec((1, LANES), lambda i: (0, 0))],
        out_specs=(pl.BlockSpec((tm, LANES), lambda i: (i, 0)),
                   pl.BlockSpec((tm, LANES), lambda i: (i, 0))),
        compiler_params=_params("parallel"),
        name="rope_tables",
    )(positions.reshape(s, 1), invf, sgn)


def _qkv_kernel(x_ref, w_ref, cos_ref, sin_ref, o_ref, *, rope_tiles):
    acc = jnp.dot(x_ref[...], w_ref[...], preferred_element_type=F32)
    j = pl.program_id(1)

    @pl.when(j < rope_tiles)
    def _():
        cos = cos_ref[...]
        sin = sin_ref[...]
        lane = lax.broadcasted_iota(jnp.int32, cos.shape, 1)
        half = ROPE_DIM // 2
        for c in range(acc.shape[1] // LANES):
            t = acc[:, c * LANES:(c + 1) * LANES]
            partner = jnp.where(lane < half, pltpu.roll(t, LANES - half, 1), pltpu.roll(t, half, 1))
            o_ref[:, c * LANES:(c + 1) * LANES] = (t * cos + partner * sin).astype(o_ref.dtype)

    @pl.when(j >= rope_tiles)
    def _():
        o_ref[...] = acc.astype(o_ref.dtype)


def _qkv_proj(xb, w, cos, sin):
    s, k = xb.shape
    n = w.shape[1]
    tm, tn = _tile(s, 1024), 512
    return pl.pallas_call(
        functools.partial(_qkv_kernel, rope_tiles=2 * ATTN_WIDTH // tn),
        out_shape=jax.ShapeDtypeStruct((s, n), BF16),
        grid=(s // tm, n // tn),
        in_specs=[pl.BlockSpec((tm, k), lambda i, j: (i, 0)),
                  pl.BlockSpec((k, tn), lambda i, j: (0, j)),
                  pl.BlockSpec((tm, LANES), lambda i, j: (i, 0)),
                  pl.BlockSpec((tm, LANES), lambda i, j: (i, 0))],
        out_specs=pl.BlockSpec((tm, tn), lambda i, j: (i, j)),
        compiler_params=_params("parallel", "arbitrary"),
        name="qkv_proj",
    )(xb, w, cos, sin)


def _mm_kernel(x_ref, w_ref, o_ref):
    o_ref[...] = jnp.dot(x_ref[...], w_ref[...], preferred_element_type=F32).astype(o_ref.dtype)


def _matmul(xb, w, out_dtype, tn):
    s, k = xb.shape
    n = w.shape[1]
    tm = _tile(s, 1024)
    return pl.pallas_call(
        _mm_kernel,
        out_shape=jax.ShapeDtypeStruct((s, n), out_dtype),
        grid=(s // tm, n // tn),
        in_specs=[pl.BlockSpec((tm, k), lambda i, j: (i, 0)),
                  pl.BlockSpec((k, tn), lambda i, j: (0, j))],
        out_specs=pl.BlockSpec((tm, tn), lambda i, j: (i, j)),
        compiler_params=_params("parallel", "arbitrary"),
        name="zx_proj",
    )(xb, w)


MOBA_GROUP = 4
MOBA_HEADS_PER_STEP = 2
ONES_ROWS = 16
LOG2_E = 1.4426950408889634


def _moba_kernel(q_ref, k_ref, v_ref, o_ref, kmean_ref, vt_ref, sel_ref, *, nb):
    bs = MOBA_BLOCK
    dh = ATTN_HEAD_DIM
    grp = MOBA_GROUP
    heads = MOBA_HEADS_PER_STEP
    qi = pl.program_id(1)
    c = dh ** -0.5 * LOG2_E
    ones = jnp.ones((ONES_ROWS, bs), F32)

    def v_transposed(rows, cols):
        return v_ref[rows, cols].astype(F32).T

    @pl.when(qi == 0)
    def _():
        kmean_ref[...] = jnp.zeros_like(kmean_ref)
        vt_ref[:, :, dh:, :] = jnp.ones((heads, nb // grp, ONES_ROWS, grp * bs), BF16)

        def body(t, carry):
            for hh in range(heads):
                cols = slice(hh * dh, (hh + 1) * dh)
                for u in range(grp):
                    n = t * grp + u
                    rows = pl.ds(pl.multiple_of(n * bs, bs), bs)
                    kmean_ref[hh, pl.ds(n, 1), :] = jnp.mean(k_ref[rows, cols].astype(F32), axis=0, keepdims=True)
                    vt_ref[hh, t, 0:dh, u * bs:(u + 1) * bs] = v_transposed(rows, cols).astype(BF16)
            return carry

        lax.fori_loop(0, nb // grp, body, 0)

    own_rows = pl.ds(pl.multiple_of(qi * bs, bs), bs)
    q_ts, carry0 = [], []
    for hh in range(heads):
        cols = slice(hh * dh, (hh + 1) * dh)
        q_t = q_ref[:, cols].astype(F32).T.astype(BF16)
        q_ts.append(q_t)

        gate = jnp.dot(kmean_ref[hh].astype(BF16), q_t, preferred_element_type=F32)
        blk = lax.broadcasted_iota(jnp.int32, gate.shape, 0)
        g = jnp.where(blk < qi, gate, -jnp.inf)
        sel = jnp.zeros(gate.shape, F32)
        for _ in range(MOBA_TOPK):
            mx = jnp.max(g, axis=0, keepdims=True)
            first = jnp.min(jnp.where(g == mx, blk, nb), axis=0, keepdims=True)
            pick = blk == jnp.where(mx > -jnp.inf, first, -1)
            sel = jnp.where(pick, 1.0, sel)
            g = jnp.where(pick, -jnp.inf, g)
        sel_ref[hh] = sel

        s = jnp.dot(k_ref[own_rows, cols], q_t, preferred_element_type=F32)
        row = lax.broadcasted_iota(jnp.int32, s.shape, 0)
        col = lax.broadcasted_iota(jnp.int32, s.shape, 1)
        s = jnp.where(row <= col, s, NEG_BIG)
        m0 = jnp.max(s, axis=0, keepdims=True)
        p = jnp.exp2((s - m0) * c)
        vt_own = jnp.concatenate([v_transposed(own_rows, cols), ones], axis=0).astype(BF16)
        carry0.append((m0, jnp.dot(vt_own, p.astype(BF16), preferred_element_type=F32)))

    def body(t, carry):
        keys = pl.ds(pl.multiple_of(t * (grp * bs), grp * bs), grp * bs)
        scores = [jnp.dot(k_ref[keys, hh * dh:(hh + 1) * dh], q_ts[hh], preferred_element_type=F32)
                  for hh in range(heads)]
        probs, stats = [], []
        for hh in range(heads):
            m = carry[hh][0]
            parts = []
            for u in range(grp):
                picked = sel_ref[hh, pl.ds(t * grp + u, 1), :] > 0.0
                parts.append(jnp.where(picked, scores[hh][u * bs:(u + 1) * bs], NEG_BIG))
            mx = parts[0]
            for u in range(1, grp):
                mx = jnp.maximum(mx, parts[u])
            m_new = jnp.maximum(m, jnp.max(mx, axis=0, keepdims=True))
            stats.append((m_new, jnp.exp2((m - m_new) * c)))
            probs.append(jnp.concatenate([jnp.exp2((part - m_new) * c).astype(BF16) for part in parts], axis=0))
        out = []
        for hh in range(heads):
            pv = jnp.dot(vt_ref[hh, t], probs[hh], preferred_element_type=F32)
            m_new, alpha = stats[hh]
            out.append((m_new, alpha * carry[hh][1] + pv))
        return tuple(out)

    trips = (qi + grp - 1) // grp
    final = lax.fori_loop(0, trips, body, tuple(carry0))
    for hh in range(heads):
        acc = final[hh][1]
        o_ref[:, hh * dh:(hh + 1) * dh] = (acc[0:dh] / acc[dh:dh + 1]).T.astype(o_ref.dtype)


def _moba_attention(qkv):
    s = qkv.shape[0]
    bs = MOBA_BLOCK
    nb = s // bs
    heads = MOBA_HEADS_PER_STEP
    assert s % bs == 0 and nb % MOBA_GROUP == 0 and ATTN_HEADS % heads == 0
    nbp = -(-nb // SUBLANES) * SUBLANES
    width = heads * ATTN_HEAD_DIM
    steps = ATTN_HEADS // heads
    return pl.pallas_call(
        functools.partial(_moba_kernel, nb=nb),
        out_shape=jax.ShapeDtypeStruct((s, ATTN_WIDTH), F32),
        grid=(steps, nb),
        in_specs=[pl.BlockSpec((bs, width), lambda h, i: (i, h)),
                  pl.BlockSpec((s, width), lambda h, i: (0, steps + h)),
                  pl.BlockSpec((s, width), lambda h, i: (0, 2 * steps + h))],
        out_specs=pl.BlockSpec((bs, width), lambda h, i: (i, h)),
        scratch_shapes=[pltpu.VMEM((heads, nbp, ATTN_HEAD_DIM), F32),
                        pltpu.VMEM((heads, nb // MOBA_GROUP, ATTN_HEAD_DIM + ONES_ROWS, MOBA_GROUP * bs), BF16),
                        pltpu.VMEM((heads, nbp, bs), F32)],
        compiler_params=_params("parallel", "arbitrary"),
        name="moba_attention",
    )(qkv, qkv, qkv)


def _silu(x):
    return x * jax.nn.sigmoid(x)


def _ssd_kernel(z_ref, xs_ref, bc_ref, dt_ref, cwx_ref, cwbc_ref, cbx_ref, cbbc_ref, dtb_ref, alog_ref,
                dskip_ref, ng_ref, expand_ref, o_ref, xpad_ref, bcpad_ref, state_ref, y_ref):
    L = SSD_CHUNK
    P = SSD_HEAD_DIM
    N = SSD_STATE
    halo = SUBLANES
    c = pl.program_id(0)
    hi = lax.Precision.HIGHEST

    @pl.when(c == 0)
    def _():
        xpad_ref[0:halo, :] = jnp.zeros((halo, xpad_ref.shape[1]), F32)
        bcpad_ref[0:halo, :] = jnp.zeros((halo, bcpad_ref.shape[1]), F32)
        state_ref[...] = jnp.zeros_like(state_ref)

    xpad_ref[halo:halo + L, :] = xs_ref[...]
    bcpad_ref[halo:halo + L, :] = bc_ref[...]

    def conv(pad_ref, w_ref, b_ref):
        acc = b_ref[...]
        for k in range(SSD_CONV):
            off = halo - (SSD_CONV - 1) + k
            acc = acc + pad_ref[off:off + L, :] * w_ref[k:k + 1, :]
        return _silu(acc)

    xs = conv(xpad_ref, cwx_ref, cbx_ref)
    bc = conv(bcpad_ref, cwbc_ref, cbbc_ref)
    xpad_ref[0:halo, :] = xpad_ref[L:L + halo, :]
    bcpad_ref[0:halo, :] = bcpad_ref[L:L + halo, :]

    lane = lax.broadcasted_iota(jnp.int32, (1, LANES), 1)
    dtr = dt_ref[...] + dtb_ref[...]
    dt = jnp.maximum(dtr, 0.0) + jnp.log1p(jnp.exp(-jnp.abs(dtr)))
    a = jnp.where(lane < SSD_HEADS, -jnp.exp(alog_ref[...]), 0.0)
    da = dt * a
    row = lax.broadcasted_iota(jnp.int32, (L, L), 0)
    col = lax.broadcasted_iota(jnp.int32, (L, L), 1)
    tril = row >= col
    a_cs = jnp.dot(tril.astype(F32), da, preferred_element_type=F32, precision=hi)
    a_cs_t = a_cs.T
    exp_a = jnp.exp(a_cs)
    decay = jnp.exp(a_cs[L - 1:L, :] - a_cs)
    stacked = jnp.concatenate([dt, exp_a, decay], axis=0)
    wide = jnp.dot(stacked, expand_ref[...], preferred_element_type=F32, precision=hi)
    dt_w, exp_w, dec_w = wide[0:L], wide[L:2 * L], wide[2 * L:3 * L]
    xdt = xs * dt_w
    xdec_b = (xdt * dec_w).astype(BF16)
    lane_w = lax.broadcasted_iota(jnp.int32, (L, LANES), 1)
    first_head = lane_w < P

    gw = SSD_WIDTH // SSD_GROUPS
    pairs_per_group = gw // LANES
    for g in range(SSD_GROUPS):
        b_g = bc[:, g * N:(g + 1) * N]
        c_g = bc[:, SSD_GROUPS * N + g * N:SSD_GROUPS * N + (g + 1) * N].astype(BF16)
        b_t = b_g.T.astype(BF16)
        cb = jnp.dot(c_g, b_t, preferred_element_type=F32)
        ssq = jnp.zeros((L, 1), F32)
        for i in range(pairs_per_group):
            pi = g * pairs_per_group + i
            blk = slice(pi * LANES, (pi + 1) * LANES)
            x_blk = xdt[:, blk]
            y_pair = jnp.zeros((L, LANES), F32)
            for e in range(2):
                h = 2 * pi + e
                seg = jnp.exp(jnp.where(tril, a_cs[:, h:h + 1] - a_cs_t[h:h + 1, :], -jnp.inf))
                mat = (cb * seg).astype(BF16)
                keep = first_head if e == 0 else jnp.logical_not(first_head)
                y_pair = y_pair + jnp.dot(mat, jnp.where(keep, x_blk, 0.0).astype(BF16),
                                          preferred_element_type=F32)
            st = jnp.dot(b_t, xdec_b[:, blk], preferred_element_type=F32)
            h_prev = state_ref[pi]
            y_off = jnp.dot(c_g, h_prev.astype(BF16), preferred_element_type=F32) * exp_w[:, blk]
            state_ref[pi] = h_prev * exp_w[L - 1:L, blk] + st
            y = y_pair + y_off + xs[:, blk] * dskip_ref[:, blk]
            yg = y * _silu(z_ref[:, blk])
            ssq = ssq + jnp.sum(yg * yg, axis=-1, keepdims=True)
            y_ref[:, blk] = yg
        rs = lax.rsqrt(ssq / gw + RMS_EPS)
        gsl = slice(g * gw, (g + 1) * gw)
        o_ref[:, gsl] = (y_ref[:, gsl] * rs * ng_ref[:, gsl]).astype(o_ref.dtype)


def _pad_lanes(v, width=LANES):
    return jnp.pad(v.astype(F32), (0, width - v.shape[0])).reshape(1, width)


def _ssd(zx, conv_w, conv_b, dt_bias, a_log, d_skip, norm_g):
    s = zx.shape[0]
    L = SSD_CHUNK
    assert s % L == 0
    w = SSD_WIDTH
    expand = (jnp.arange(LANES)[:, None] == (jnp.arange(w)[None, :] // SSD_HEAD_DIM)).astype(F32)
    d_wide = jnp.repeat(d_skip.astype(F32), SSD_HEAD_DIM).reshape(1, w)
    row1 = lambda width: pl.BlockSpec((1, width), lambda c: (0, 0))
    return pl.pallas_call(
        _ssd_kernel,
        out_shape=jax.ShapeDtypeStruct((s, w), BF16),
        grid=(s // L,),
        in_specs=[pl.BlockSpec((L, w), lambda c: (c, 0)),
                  pl.BlockSpec((L, w), lambda c: (c, 1)),
                  pl.BlockSpec((L, SSD_BC), lambda c: (c, 2 * w // SSD_BC)),
                  pl.BlockSpec((L, DT_PAD), lambda c: (c, (2 * w + SSD_BC) // DT_PAD)),
                  pl.BlockSpec((SSD_CONV, w), lambda c: (0, 0)),
                  pl.BlockSpec((SSD_CONV, SSD_BC), lambda c: (0, 0)),
                  row1(w), row1(SSD_BC), row1(LANES), row1(LANES), row1(w), row1(w),
                  pl.BlockSpec((LANES, w), lambda c: (0, 0))],
        out_specs=pl.BlockSpec((L, w), lambda c: (c, 0)),
        scratch_shapes=[pltpu.VMEM((L + SUBLANES, w), F32),
                        pltpu.VMEM((L + SUBLANES, SSD_BC), F32),
                        pltpu.VMEM((w // LANES, SSD_STATE, LANES), F32),
                        pltpu.VMEM((L, w), F32)],
        compiler_params=_params("arbitrary"),
        name="ssd_scan",
    )(zx, zx, zx, zx, conv_w[:, :w], conv_w[:, w:], conv_b[:w].reshape(1, w), conv_b[w:].reshape(1, SSD_BC),
      _pad_lanes(dt_bias), _pad_lanes(a_log), d_wide, norm_g.reshape(1, w), expand)


def _layer_norm(y, g, b):
    mu = jnp.mean(y, axis=-1, keepdims=True)
    d = y - mu
    var = jnp.mean(d * d, axis=-1, keepdims=True)
    return d * lax.rsqrt(var + LN_EPS) * g + b


def _outproj_kernel(attn_ref, ssd_ref, h_ref, w_ref, ag_ref, g_ref, b_ref, o_ref, ob_ref):
    attn = attn_ref[...]
    attn_n = attn * lax.rsqrt(jnp.mean(attn * attn, axis=-1, keepdims=True) + RMS_EPS) * ag_ref[...]
    aw = ATTN_WIDTH
    mix = jnp.dot(attn_n.astype(BF16), w_ref[0:aw, :], preferred_element_type=F32)
    mix = mix + jnp.dot(ssd_ref[...], w_ref[aw:, :], preferred_element_type=F32)
    out = _layer_norm(DEEPNORM_ALPHA * h_ref[...] + mix, g_ref[...], b_ref[...])
    o_ref[...] = out
    ob_ref[...] = out.astype(BF16)


def _outproj_ln(attn, ssd, h, w, attn_g, ln_g, ln_b):
    s, d = h.shape
    tm = _tile(s, 512)
    row = lambda width: pl.BlockSpec((1, width), lambda i: (0, 0))
    return pl.pallas_call(
        _outproj_kernel,
        out_shape=(jax.ShapeDtypeStruct((s, d), F32), jax.ShapeDtypeStruct((s, d), BF16)),
        grid=(s // tm,),
        in_specs=[pl.BlockSpec((tm, ATTN_WIDTH), lambda i: (i, 0)),
                  pl.BlockSpec((tm, SSD_WIDTH), lambda i: (i, 0)),
                  pl.BlockSpec((tm, d), lambda i: (i, 0)),
                  pl.BlockSpec((d, d), lambda i: (0, 0)),
                  row(ATTN_WIDTH), row(d), row(d)],
        out_specs=(pl.BlockSpec((tm, d), lambda i: (i, 0)), pl.BlockSpec((tm, d), lambda i: (i, 0))),
        compiler_params=_params("parallel"),
        name="outproj_ln",
    )(attn, ssd, h, w, attn_g.reshape(1, -1), ln_g.reshape(1, -1), ln_b.reshape(1, -1))


def _ffn_up_kernel(x_ref, wg_ref, wu_ref, cwg_ref, cwu_ref, cbg_ref, cbu_ref, o_ref, gpad_ref, upad_ref):
    tm = x_ref.shape[0]
    halo = SUBLANES
    i = pl.program_id(1)

    @pl.when(i == 0)
    def _():
        gpad_ref[0:halo, :] = jnp.zeros((halo, gpad_ref.shape[1]), F32)
        upad_ref[0:halo, :] = jnp.zeros((halo, upad_ref.shape[1]), F32)

    x = x_ref[...]
    gpad_ref[halo:halo + tm, :] = jnp.dot(x, wg_ref[...], preferred_element_type=F32)
    upad_ref[halo:halo + tm, :] = jnp.dot(x, wu_ref[...], preferred_element_type=F32)

    def conv(pad_ref, w_ref, b_ref):
        acc = b_ref[...]
        for k in range(FFN_CONV):
            off = halo - (FFN_CONV - 1) + k
            acc = acc + pad_ref[off:off + tm, :] * w_ref[k:k + 1, :]
        return acc

    g = conv(gpad_ref, cwg_ref, cbg_ref)
    u = conv(upad_ref, cwu_ref, cbu_ref)
    o_ref[...] = (_silu(g) * u).astype(o_ref.dtype)
    gpad_ref[0:halo, :] = gpad_ref[tm:tm + halo, :]
    upad_ref[0:halo, :] = upad_ref[tm:tm + halo, :]


def _ffn_up(hb, w_up, conv_w, conv_b):
    s, d = hb.shape
    tm, tn = _tile(s, 1024), 512
    nj = D_FF // tn
    conv_b = conv_b.reshape(1, -1)
    return pl.pallas_call(
        _ffn_up_kernel,
        out_shape=jax.ShapeDtypeStruct((s, D_FF), BF16),
        grid=(nj, s // tm),
        in_specs=[pl.BlockSpec((tm, d), lambda j, i: (i, 0)),
                  pl.BlockSpec((d, tn), lambda j, i: (0, j)),
                  pl.BlockSpec((d, tn), lambda j, i: (0, j + nj)),
                  pl.BlockSpec((FFN_CONV, tn), lambda j, i: (0, j)),
                  pl.BlockSpec((FFN_CONV, tn), lambda j, i: (0, j + nj)),
                  pl.BlockSpec((1, tn), lambda j, i: (0, j)),
                  pl.BlockSpec((1, tn), lambda j, i: (0, j + nj))],
        out_specs=pl.BlockSpec((tm, tn), lambda j, i: (i, j)),
        scratch_shapes=[pltpu.VMEM((tm + SUBLANES, tn), F32), pltpu.VMEM((tm + SUBLANES, tn), F32)],
        compiler_params=_params("parallel", "arbitrary"),
        name="ffn_up",
    )(hb, w_up, w_up, conv_w, conv_w, conv_b, conv_b)


def _ple_kernel(p_ref, hb_ref, wp_ref, wg_ref, o_ref):
    proj = jnp.dot(p_ref[...].astype(BF16), wp_ref[...], preferred_element_type=F32)
    gate = jnp.dot(hb_ref[...], wg_ref[...], preferred_element_type=F32)
    o_ref[...] = proj * jax.nn.sigmoid(gate)


def _ple(p, hb, w_proj, w_gate):
    s, d = hb.shape
    tm, tn = _tile(s, 1024), 1024
    return pl.pallas_call(
        _ple_kernel,
        out_shape=jax.ShapeDtypeStruct((s, d), F32),
        grid=(s // tm, d // tn),
        in_specs=[pl.BlockSpec((tm, PE_DIM), lambda i, j: (i, 0)),
                  pl.BlockSpec((tm, d), lambda i, j: (i, 0)),
                  pl.BlockSpec((PE_DIM, tn), lambda i, j: (0, j)),
                  pl.BlockSpec((d, tn), lambda i, j: (0, j))],
        out_specs=pl.BlockSpec((tm, tn), lambda i, j: (i, j)),
        compiler_params=_params("parallel", "arbitrary"),
        name="ple_gate",
    )(p, hb, w_proj, w_gate)


def _down_kernel(a_ref, w_ref, h_ref, ple_ref, g_ref, b_ref, o_ref, ob_ref, acc_ref):
    k = pl.program_id(1)

    @pl.when(k == 0)
    def _():
        acc_ref[...] = jnp.zeros_like(acc_ref)

    acc_ref[...] += jnp.dot(a_ref[...], w_ref[...], preferred_element_type=F32)

    @pl.when(k == pl.num_programs(1) - 1)
    def _():
        out = _layer_norm(DEEPNORM_ALPHA * h_ref[...] + acc_ref[...] + ple_ref[...], g_ref[...], b_ref[...])
        o_ref[...] = out
        ob_ref[...] = out.astype(BF16)


def _down_ln(act, w_down, h, ple, ln_g, ln_b):
    s, d = h.shape
    tm, tk = _tile(s, 512), D_FF // 4
    row = lambda width: pl.BlockSpec((1, width), lambda i, k: (0, 0))
    return pl.pallas_call(
        _down_kernel,
        out_shape=(jax.ShapeDtypeStruct((s, d), F32), jax.ShapeDtypeStruct((s, d), BF16)),
        grid=(s // tm, D_FF // tk),
        in_specs=[pl.BlockSpec((tm, tk), lambda i, k: (i, k)),
                  pl.BlockSpec((tk, d), lambda i, k: (k, 0)),
                  pl.BlockSpec((tm, d), lambda i, k: (i, 0)),
                  pl.BlockSpec((tm, d), lambda i, k: (i, 0)),
                  row(d), row(d)],
        out_specs=(pl.BlockSpec((tm, d), lambda i, k: (i, 0)), pl.BlockSpec((tm, d), lambda i, k: (i, 0))),
        scratch_shapes=[pltpu.VMEM((tm, d), F32)],
        compiler_params=_params("parallel", "arbitrary"),
        name="down_ln",
    )(act, w_down, h, ple, ln_g.reshape(1, -1), ln_b.reshape(1, -1))


def _layer(h, hb, p_i, cos, sin, w_in, attn_norm_g, ssd_conv_w, ssd_conv_b, ssd_dt_bias, ssd_a_log, ssd_d,
           ssd_norm_g, w_out, ln1_g, ln1_b, w_up, ffn_conv_w, ffn_conv_b, w_down, w_pe_gate, w_pe_proj,
           ln2_g, ln2_b):
    qkv_w = 3 * ATTN_WIDTH
    w_qkv = w_in[:, :qkv_w].astype(BF16)
    n_dt = w_in.shape[1] - qkv_w - SSD_WIDTH - SSD_CONV_DIM
    w_zx = jnp.pad(w_in[:, qkv_w:], ((0, 0), (0, DT_PAD - n_dt))).astype(BF16)
    qkv = _qkv_proj(hb, w_qkv, cos, sin)
    zx = _matmul(hb, w_zx, F32, ZX_WIDTH // 3)
    attn = _moba_attention(qkv)
    ssd = _ssd(zx, ssd_conv_w, ssd_conv_b, ssd_dt_bias, ssd_a_log, ssd_d, ssd_norm_g)
    h1, h1b = _outproj_ln(attn, ssd, h, w_out.astype(BF16), attn_norm_g, ln1_g, ln1_b)
    act = _ffn_up(h1b, w_up.astype(BF16), ffn_conv_w, ffn_conv_b)
    ple = _ple(p_i, h1b, w_pe_proj.astype(BF16), w_pe_gate.astype(BF16))
    return _down_ln(act, w_down.astype(BF16), h1, ple, ln2_g, ln2_b)


def kernel(x, p, positions, w_in, attn_norm_g, ssd_conv_w, ssd_conv_b, ssd_dt_bias, ssd_a_log, ssd_d, ssd_norm_g,
           w_out, ln1_g, ln1_b, w_up, ffn_conv_w, ffn_conv_b, w_down, w_pe_gate, w_pe_proj, ln2_g, ln2_b):
    b, s, d = x.shape
    assert b == 1 and d == D_MODEL
    cos, sin = _rope_tables(positions[0])
    h = x[0]
    hb = h.astype(BF16)
    for i in range(w_in.shape[0]):
        h, hb = _layer(h, hb, p[i, 0], cos, sin, w_in[i], attn_norm_g[i], ssd_conv_w[i], ssd_conv_b[i],
                       ssd_dt_bias[i], ssd_a_log[i], ssd_d[i], ssd_norm_g[i], w_out[i], ln1_g[i], ln1_b[i],
                       w_up[i], ffn_conv_w[i], ffn_conv_b[i], w_down[i], w_pe_gate[i], w_pe_proj[i],
                       ln2_g[i], ln2_b[i])
    return h[None]
```

```python
import functools

import jax
import jax.numpy as jnp
from jax import lax
from jax.experimental import pallas as pl
from jax.experimental.pallas import tpu as pltpu

F32 = jnp.float32
BF16 = jnp.bfloat16

D_MODEL = 2048
DEPTH = 4
PE_DIM = 256
ATTN_HEADS = 8
ATTN_HEAD_DIM = 128
ATTN_WIDTH = ATTN_HEADS * ATTN_HEAD_DIM
MOBA_BLOCK = 256
MOBA_TOPK = 3
ROPE_DIM = ATTN_HEAD_DIM // 4
ROPE_THETA = 500000.0
SSD_WIDTH = D_MODEL - ATTN_WIDTH
SSD_HEAD_DIM = 64
SSD_HEADS = SSD_WIDTH // SSD_HEAD_DIM
SSD_GROUPS = 2
SSD_STATE = 128
SSD_CONV = 4
SSD_CHUNK = 256
SSD_BC = 2 * SSD_GROUPS * SSD_STATE
SSD_CONV_DIM = SSD_WIDTH + SSD_BC
D_FF = 5632
FFN_CONV = 3
DEEPNORM_ALPHA = (2 * DEPTH) ** 0.25
LN_EPS = 1e-5
RMS_EPS = 1e-6

LANES = 128
SUBLANES = 8
VMEM_LIMIT = 56 * 1024 * 1024

NEG_BIG = -1e30
DT_PAD = LANES
ZX_WIDTH = SSD_WIDTH + SSD_CONV_DIM


def _params(*sem):
    return pltpu.CompilerParams(dimension_semantics=sem, vmem_limit_bytes=VMEM_LIMIT)


def _tile(n, pref):
    t = min(n, pref)
    assert n % t == 0, (n, t)
    return t


def _rope_table_kernel(pos_ref, invf_ref, sgn_ref, cos_ref, sin_ref):
    ang = pos_ref[...].astype(F32) * invf_ref[...]
    cos_ref[...] = jnp.cos(ang)
    sin_ref[...] = jnp.sin(ang) * sgn_ref[...]


def _rope_tables(positions):
    s = positions.shape[0]
    half = ROPE_DIM // 2
    inv_freq = 1.0 / (ROPE_THETA ** (jnp.arange(0, ROPE_DIM, 2, dtype=F32) / ROPE_DIM))
    zeros = jnp.zeros((LANES - ROPE_DIM,), F32)
    invf = jnp.concatenate([inv_freq, inv_freq, zeros]).reshape(1, LANES)
    sgn = jnp.concatenate([-jnp.ones((half,), F32), jnp.ones((half,), F32), zeros]).reshape(1, LANES)
    tm = _tile(s, 1024)
    return pl.pallas_call(
        _rope_table_kernel,
        out_shape=(jax.ShapeDtypeStruct((s, LANES), F32), jax.ShapeDtypeStruct((s, LANES), F32)),
        grid=(s // tm,),
        in_specs=[pl.BlockSpec((tm, 1), lambda i: (i, 0)),
                  pl.BlockSpec((1, LANES), lambda i: (0, 0)),
                  pl.BlockSpec((1, LANES), lambda i: (0, 0))],
        out_specs=(pl.BlockSpec((tm, LANES), lambda i: (i, 0)),
                   pl.BlockSpec((tm, LANES), lambda i: (i, 0))),
        compiler_params=_params("parallel"),
        name="rope_tables",
    )(positions.reshape(s, 1), invf, sgn)


PROJ_TN = 512
ROW_SUBTILES = 1


def _cast_weights_once(w_ref, wb_ref):
    @pl.when(pl.program_id(1) == 0)
    def _():
        wb_ref[...] = w_ref[...].astype(BF16)


def _row_subtiles(tm):
    sub = tm // ROW_SUBTILES
    return [slice(r * sub, (r + 1) * sub) for r in range(ROW_SUBTILES)]


def _qkv_kernel(x_ref, w_ref, cos_ref, sin_ref, o_ref, wb_ref, *, rope_tiles):
    _cast_weights_once(w_ref, wb_ref)
    j = pl.program_id(0)
    subs = _row_subtiles(x_ref.shape[0])
    accs = [jnp.dot(x_ref[rows, :], wb_ref[...], preferred_element_type=F32) for rows in subs]

    @pl.when(j < rope_tiles)
    def _():
        half = ROPE_DIM // 2
        for rows, acc in zip(subs, accs):
            cos = cos_ref[rows, :]
            sin = sin_ref[rows, :]
            lane = lax.broadcasted_iota(jnp.int32, cos.shape, 1)
            for c in range(acc.shape[1] // LANES):
                t = acc[:, c * LANES:(c + 1) * LANES]
                partner = jnp.where(lane < half, pltpu.roll(t, LANES - half, 1), pltpu.roll(t, half, 1))
                o_ref[rows, c * LANES:(c + 1) * LANES] = (t * cos + partner * sin).astype(o_ref.dtype)

    @pl.when(j >= rope_tiles)
    def _():
        for rows, acc in zip(subs, accs):
            o_ref[rows, :] = acc.astype(o_ref.dtype)


def _qkv_proj(xb, w_in, layer, cos, sin):
    s, k = xb.shape
    n = 3 * ATTN_WIDTH
    tm, tn = _tile(s, 2048), PROJ_TN
    return pl.pallas_call(
        functools.partial(_qkv_kernel, rope_tiles=2 * ATTN_WIDTH // tn),
        out_shape=jax.ShapeDtypeStruct((s, n), BF16),
        grid=(n // tn, s // tm),
        in_specs=[pl.BlockSpec((tm, k), lambda j, i: (i, 0)),
                  pl.BlockSpec((None, k, tn), lambda j, i: (layer, 0, j)),
                  pl.BlockSpec((tm, LANES), lambda j, i: (i, 0)),
                  pl.BlockSpec((tm, LANES), lambda j, i: (i, 0))],
        out_specs=pl.BlockSpec((tm, tn), lambda j, i: (i, j)),
        scratch_shapes=[pltpu.VMEM((k, tn), BF16)],
        compiler_params=_params("parallel", "arbitrary"),
        name="qkv_proj",
    )(xb, w_in, cos, sin)


def _zx_kernel(x_ref, w_ref, o_ref, wb_ref):
    _cast_weights_once(w_ref, wb_ref)
    for rows in _row_subtiles(x_ref.shape[0]):
        o_ref[rows, :] = jnp.dot(x_ref[rows, :], wb_ref[...], preferred_element_type=F32)


def _zx_proj(xb, w_in, layer):
    s, k = xb.shape
    n = ZX_WIDTH
    tm, tn = _tile(s, 2048), PROJ_TN
    first = 3 * ATTN_WIDTH // tn
    return pl.pallas_call(
        _zx_kernel,
        out_shape=jax.ShapeDtypeStruct((s, n), F32),
        grid=(n // tn, s // tm),
        in_specs=[pl.BlockSpec((tm, k), lambda j, i: (i, 0)),
                  pl.BlockSpec((None, k, tn), lambda j, i: (layer, 0, first + j))],
        out_specs=pl.BlockSpec((tm, tn), lambda j, i: (i, j)),
        scratch_shapes=[pltpu.VMEM((k, tn), BF16)],
        compiler_params=_params("parallel", "arbitrary"),
        name="zx_proj",
    )(xb, w_in)


MOBA_GROUP = 4
MOBA_HEADS_PER_STEP = 2
ONES_ROWS = 16
LOG2_E = 1.4426950408889634


def _moba_kernel(q_ref, k_ref, v_ref, o_ref, kmean_ref, vt_ref, sel_ref, *, nb):
    bs = MOBA_BLOCK
    dh = ATTN_HEAD_DIM
    grp = MOBA_GROUP
    heads = MOBA_HEADS_PER_STEP
    qi = pl.program_id(1)
    c = dh ** -0.5 * LOG2_E
    ones = jnp.ones((ONES_ROWS, bs), F32)

    def v_transposed(rows, cols):
        return v_ref[rows, cols].astype(F32).T

    @pl.when(qi == 0)
    def _():
        kmean_ref[...] = jnp.zeros_like(kmean_ref)
        vt_ref[:, :, dh:, :] = jnp.ones((heads, nb // grp, ONES_ROWS, grp * bs), BF16)

        def body(t, carry):
            for hh in range(heads):
                cols = slice(hh * dh, (hh + 1) * dh)
                for u in range(grp):
                    n = t * grp + u
                    rows = pl.ds(pl.multiple_of(n * bs, bs), bs)
                    kmean_ref[hh, pl.ds(n, 1), :] = jnp.mean(k_ref[rows, cols].astype(F32), axis=0, keepdims=True)
                    vt_ref[hh, t, 0:dh, u * bs:(u + 1) * bs] = v_transposed(rows, cols).astype(BF16)
            return carry

        lax.fori_loop(0, nb // grp, body, 0)

    own_rows = pl.ds(pl.multiple_of(qi * bs, bs), bs)
    q_ts, carry0 = [], []
    for hh in range(heads):
        cols = slice(hh * dh, (hh + 1) * dh)
        q_t = q_ref[:, cols].astype(F32).T.astype(BF16)
        q_ts.append(q_t)

        gate = jnp.dot(kmean_ref[hh].astype(BF16), q_t, preferred_element_type=F32)
        blk = lax.broadcasted_iota(jnp.int32, gate.shape, 0)
        g = jnp.where(blk < qi, gate, -jnp.inf)
        sel = jnp.zeros(gate.shape, F32)
        for _ in range(MOBA_TOPK):
            mx = jnp.max(g, axis=0, keepdims=True)
            first = jnp.min(jnp.where(g == mx, blk, nb), axis=0, keepdims=True)
            pick = blk == jnp.where(mx > -jnp.inf, first, -1)
            sel = jnp.where(pick, 1.0, sel)
            g = jnp.where(pick, -jnp.inf, g)
        sel_ref[hh] = sel

        s = jnp.dot(k_ref[own_rows, cols], q_t, preferred_element_type=F32)
        row = lax.broadcasted_iota(jnp.int32, s.shape, 0)
        col = lax.broadcasted_iota(jnp.int32, s.shape, 1)
        s = jnp.where(row <= col, s, NEG_BIG)
        m0 = jnp.max(s, axis=0, keepdims=True)
        p = jnp.exp2((s - m0) * c)
        vt_own = jnp.concatenate([v_transposed(own_rows, cols), ones], axis=0).astype(BF16)
        carry0.append((m0, jnp.dot(vt_own, p.astype(BF16), preferred_element_type=F32)))

    def body(t, state):
        keys = pl.ds(pl.multiple_of(t * (grp * bs), grp * bs), grp * bs)
        scores = [jnp.dot(k_ref[keys, hh * dh:(hh + 1) * dh], q_ts[hh], preferred_element_type=F32)
                  for hh in range(heads)]
        probs, stats = [], []
        for hh in range(heads):
            m = state[hh][0]
            parts = []
            for u in range(grp):
                picked = sel_ref[hh, pl.ds(t * grp + u, 1), :] > 0.0
                parts.append(jnp.where(picked, scores[hh][u * bs:(u + 1) * bs], NEG_BIG))
            mx = parts[0]
            for u in range(1, grp):
                mx = jnp.maximum(mx, parts[u])
            m_new = jnp.maximum(m, jnp.max(mx, axis=0, keepdims=True))
            stats.append((m_new, jnp.exp2((m - m_new) * c)))
            probs.append(jnp.concatenate([jnp.exp2((part - m_new) * c).astype(BF16) for part in parts], axis=0))
        out = []
        for hh in range(heads):
            pv = jnp.dot(vt_ref[hh, t], probs[hh], preferred_element_type=F32)
            m_new, alpha = stats[hh]
            out.append((m_new, alpha * state[hh][1] + pv))
        return tuple(out)

    trips = (qi + grp - 1) // grp
    final = lax.fori_loop(0, trips, body, tuple(carry0))
    for hh in range(heads):
        acc = final[hh][1]
        o_ref[:, hh * dh:(hh + 1) * dh] = (acc[0:dh] / acc[dh:dh + 1]).T.astype(o_ref.dtype)


def _moba_attention(qkv):
    s = qkv.shape[0]
    bs = MOBA_BLOCK
    nb = s // bs
    heads = MOBA_HEADS_PER_STEP
    assert s % bs == 0 and nb % MOBA_GROUP == 0 and ATTN_HEADS % heads == 0
    nbp = -(-nb // SUBLANES) * SUBLANES
    width = heads * ATTN_HEAD_DIM
    steps = ATTN_HEADS // heads
    return pl.pallas_call(
        functools.partial(_moba_kernel, nb=nb),
        out_shape=jax.ShapeDtypeStruct((s, ATTN_WIDTH), F32),
        grid=(steps, nb),
        in_specs=[pl.BlockSpec((bs, width), lambda h, i: (i, h)),
                  pl.BlockSpec((s, width), lambda h, i: (0, steps + h)),
                  pl.BlockSpec((s, width), lambda h, i: (0, 2 * steps + h))],
        out_specs=pl.BlockSpec((bs, width), lambda h, i: (i, h)),
        scratch_shapes=[pltpu.VMEM((heads, nbp, ATTN_HEAD_DIM), F32),
                        pltpu.VMEM((heads, nb // MOBA_GROUP, ATTN_HEAD_DIM + ONES_ROWS, MOBA_GROUP * bs), BF16),
                        pltpu.VMEM((heads, nbp, bs), F32)],
        compiler_params=_params("parallel", "arbitrary"),
        name="moba_attention",
    )(qkv, qkv, qkv)


def _silu(x):
    return x * jax.nn.sigmoid(x)


def _ssd_kernel(z_ref, xs_ref, bc_ref, hb_ref, wdt_ref, cwx_ref, cwbc_ref, cbx_ref, cbbc_ref, dtb_ref, alog_ref,
                dskip_ref, ng_ref, expand_ref, o_ref, xpad_ref, bcpad_ref, state_ref, y_ref):
    L = SSD_CHUNK
    P = SSD_HEAD_DIM
    N = SSD_STATE
    halo = SUBLANES
    c = pl.program_id(0)
    hi = lax.Precision.HIGHEST

    @pl.when(c == 0)
    def _():
        xpad_ref[0:halo, :] = jnp.zeros((halo, xpad_ref.shape[1]), F32)
        bcpad_ref[0:halo, :] = jnp.zeros((halo, bcpad_ref.shape[1]), F32)
        state_ref[...] = jnp.zeros_like(state_ref)

    xpad_ref[halo:halo + L, :] = xs_ref[...]
    bcpad_ref[halo:halo + L, :] = bc_ref[...]

    def conv(pad_ref, w_ref, b_ref):
        acc = b_ref[...]
        for k in range(SSD_CONV):
            off = halo - (SSD_CONV - 1) + k
            acc = acc + pad_ref[off:off + L, :] * w_ref[k:k + 1, :]
        return _silu(acc)

    xs = conv(xpad_ref, cwx_ref, cbx_ref)
    bc = conv(bcpad_ref, cwbc_ref, cbbc_ref)
    xpad_ref[0:halo, :] = xpad_ref[L:L + halo, :]
    bcpad_ref[0:halo, :] = bcpad_ref[L:L + halo, :]

    lane = lax.broadcasted_iota(jnp.int32, (1, LANES), 1)
    dtr = jnp.dot(hb_ref[...], wdt_ref[...], preferred_element_type=F32) + dtb_ref[...]
    dt = jnp.maximum(dtr, 0.0) + jnp.log1p(jnp.exp(-jnp.abs(dtr)))
    a = jnp.where(lane < SSD_HEADS, -jnp.exp(alog_ref[...]), 0.0)
    da = dt * a
    row = lax.broadcasted_iota(jnp.int32, (L, L), 0)
    col = lax.broadcasted_iota(jnp.int32, (L, L), 1)
    tril = row >= col
    a_cs = jnp.dot(tril.astype(F32), da, preferred_element_type=F32, precision=hi)
    a_cs_t = a_cs.T
    exp_a = jnp.exp(a_cs)
    decay = jnp.exp(a_cs[L - 1:L, :] - a_cs)
    stacked = jnp.concatenate([dt, exp_a, decay], axis=0)
    wide = jnp.dot(stacked, expand_ref[...], preferred_element_type=F32, precision=hi)
    dt_w, exp_w, dec_w = wide[0:L], wide[L:2 * L], wide[2 * L:3 * L]
    xdt = xs * dt_w
    xdec_b = (xdt * dec_w).astype(BF16)
    lane_w = lax.broadcasted_iota(jnp.int32, (L, LANES), 1)
    first_head = lane_w < P

    gw = SSD_WIDTH // SSD_GROUPS
    pairs_per_group = gw // LANES
    for g in range(SSD_GROUPS):
        b_g = bc[:, g * N:(g + 1) * N]
        c_g = bc[:, SSD_GROUPS * N + g * N:SSD_GROUPS * N + (g + 1) * N].astype(BF16)
        b_t = b_g.T.astype(BF16)
        cb = jnp.dot(c_g, b_t, preferred_element_type=F32)
        ssq = jnp.zeros((L, 1), F32)
        for i in range(pairs_per_group):
            pi = g * pairs_per_group + i
            blk = slice(pi * LANES, (pi + 1) * LANES)
            x_blk = xdt[:, blk]
            y_pair = jnp.zeros((L, LANES), F32)
            for e in range(2):
                h = 2 * pi + e
                seg = jnp.exp(jnp.where(tril, a_cs[:, h:h + 1] - a_cs_t[h:h + 1, :], -jnp.inf))
                mat = (cb * seg).astype(BF16)
                keep = first_head if e == 0 else jnp.logical_not(first_head)
                y_pair = y_pair + jnp.dot(mat, jnp.where(keep, x_blk, 0.0).astype(BF16),
                                          preferred_element_type=F32)
            st = jnp.dot(b_t, xdec_b[:, blk], preferred_element_type=F32)
            h_prev = state_ref[pi]
            y_off = jnp.dot(c_g, h_prev.astype(BF16), preferred_element_type=F32) * exp_w[:, blk]
            state_ref[pi] = h_prev * exp_w[L - 1:L, blk] + st
            y = y_pair + y_off + xs[:, blk] * dskip_ref[:, blk]
            yg = y * _silu(z_ref[:, blk])
            ssq = ssq + jnp.sum(yg * yg, axis=-1, keepdims=True)
            y_ref[:, blk] = yg
        rs = lax.rsqrt(ssq / gw + RMS_EPS)
        gsl = slice(g * gw, (g + 1) * gw)
        o_ref[:, gsl] = (y_ref[:, gsl] * rs * ng_ref[:, gsl]).astype(o_ref.dtype)


def _pad_lanes(v, width=LANES):
    return jnp.pad(v.astype(F32), (0, width - v.shape[0])).reshape(1, width)


def _ssd(zx, hb, w_dt, conv_w, conv_b, dt_bias, a_log, d_skip, norm_g):
    s = zx.shape[0]
    d = hb.shape[1]
    w_dt = jnp.pad(w_dt, ((0, 0), (0, DT_PAD - w_dt.shape[1]))).astype(BF16)
    L = SSD_CHUNK
    assert s % L == 0
    w = SSD_WIDTH
    expand = (jnp.arange(LANES)[:, None] == (jnp.arange(w)[None, :] // SSD_HEAD_DIM)).astype(F32)
    d_wide = jnp.repeat(d_skip.astype(F32), SSD_HEAD_DIM).reshape(1, w)
    row1 = lambda width: pl.BlockSpec((1, width), lambda c: (0, 0))
    return pl.pallas_call(
        _ssd_kernel,
        out_shape=jax.ShapeDtypeStruct((s, w), BF16),
        grid=(s // L,),
        in_specs=[pl.BlockSpec((L, w), lambda c: (c, 0)),
                  pl.BlockSpec((L, w), lambda c: (c, 1)),
                  pl.BlockSpec((L, SSD_BC), lambda c: (c, 2 * w // SSD_BC)),
                  pl.BlockSpec((L, d), lambda c: (c, 0)),
                  pl.BlockSpec((d, DT_PAD), lambda c: (0, 0)),
                  pl.BlockSpec((SSD_CONV, w), lambda c: (0, 0)),
                  pl.BlockSpec((SSD_CONV, SSD_BC), lambda c: (0, 0)),
                  row1(w), row1(SSD_BC), row1(LANES), row1(LANES), row1(w), row1(w),
                  pl.BlockSpec((LANES, w), lambda c: (0, 0))],
        out_specs=pl.BlockSpec((L, w), lambda c: (c, 0)),
        scratch_shapes=[pltpu.VMEM((L + SUBLANES, w), F32),
                        pltpu.VMEM((L + SUBLANES, SSD_BC), F32),
                        pltpu.VMEM((w // LANES, SSD_STATE, LANES), F32),
                        pltpu.VMEM((L, w), F32)],
        compiler_params=_params("arbitrary"),
        name="ssd_scan",
    )(zx, zx, zx, hb, w_dt, conv_w[:, :w], conv_w[:, w:], conv_b[:w].reshape(1, w), conv_b[w:].reshape(1, SSD_BC),
      _pad_lanes(dt_bias), _pad_lanes(a_log), d_wide, norm_g.reshape(1, w), expand)


def _layer_norm(y, g, b):
    mu = jnp.mean(y, axis=-1, keepdims=True)
    d = y - mu
    var = jnp.mean(d * d, axis=-1, keepdims=True)
    return d * lax.rsqrt(var + LN_EPS) * g + b


def _outproj_kernel(attn_ref, ssd_ref, h_ref, w_ref, ag_ref, g_ref, b_ref, o_ref, ob_ref):
    attn = attn_ref[...]
    attn_n = attn * lax.rsqrt(jnp.mean(attn * attn, axis=-1, keepdims=True) + RMS_EPS) * ag_ref[...]
    aw = ATTN_WIDTH
    mix = jnp.dot(attn_n.astype(BF16), w_ref[0:aw, :], preferred_element_type=F32)
    mix = mix + jnp.dot(ssd_ref[...], w_ref[aw:, :], preferred_element_type=F32)
    out = _layer_norm(DEEPNORM_ALPHA * h_ref[...] + mix, g_ref[...], b_ref[...])
    o_ref[...] = out
    ob_ref[...] = out.astype(BF16)


def _outproj_ln(attn, ssd, h, w, layer, attn_g, ln_g, ln_b):
    s, d = h.shape
    tm = _tile(s, 512)
    row = lambda width: pl.BlockSpec((1, width), lambda i: (0, 0))
    return pl.pallas_call(
        _outproj_kernel,
        out_shape=(jax.ShapeDtypeStruct((s, d), F32), jax.ShapeDtypeStruct((s, d), BF16)),
        grid=(s // tm,),
        in_specs=[pl.BlockSpec((tm, ATTN_WIDTH), lambda i: (i, 0)),
                  pl.BlockSpec((tm, SSD_WIDTH), lambda i: (i, 0)),
                  pl.BlockSpec((tm, d), lambda i: (i, 0)),
                  pl.BlockSpec((None, d, d), lambda i: (layer, 0, 0), pipeline_mode=pl.Buffered(1)),
                  row(ATTN_WIDTH), row(d), row(d)],
        out_specs=(pl.BlockSpec((tm, d), lambda i: (i, 0)), pl.BlockSpec((tm, d), lambda i: (i, 0))),
        compiler_params=_params("parallel"),
        name="outproj_ln",
    )(attn, ssd, h, w, attn_g.reshape(1, -1), ln_g.reshape(1, -1), ln_b.reshape(1, -1))


def _ffn_up_kernel(x_ref, wg_ref, wu_ref, cwg_ref, cwu_ref, cbg_ref, cbu_ref, o_ref, gpad_ref, upad_ref,
                   wgb_ref, wub_ref):
    tm = x_ref.shape[0]
    halo = SUBLANES
    _cast_weights_once(wg_ref, wgb_ref)
    _cast_weights_once(wu_ref, wub_ref)

    @pl.when(pl.program_id(1) == 0)
    def _():
        gpad_ref[0:halo, :] = jnp.zeros((halo, gpad_ref.shape[1]), F32)
        upad_ref[0:halo, :] = jnp.zeros((halo, upad_ref.shape[1]), F32)

    subs = _row_subtiles(tm)
    for rows in subs:
        x = x_ref[rows, :]
        padded = slice(halo + rows.start, halo + rows.stop)
        gpad_ref[padded, :] = jnp.dot(x, wgb_ref[...], preferred_element_type=F32)
        upad_ref[padded, :] = jnp.dot(x, wub_ref[...], preferred_element_type=F32)

    def conv(pad_ref, w_ref, b_ref, rows):
        acc = b_ref[...]
        for k in range(FFN_CONV):
            off = halo - (FFN_CONV - 1) + k
            acc = acc + pad_ref[off + rows.start:off + rows.stop, :] * w_ref[k:k + 1, :]
        return acc

    for rows in subs:
        g = conv(gpad_ref, cwg_ref, cbg_ref, rows)
        u = conv(upad_ref, cwu_ref, cbu_ref, rows)
        o_ref[rows, :] = (_silu(g) * u).astype(o_ref.dtype)
    gpad_ref[0:halo, :] = gpad_ref[tm:tm + halo, :]
    upad_ref[0:halo, :] = upad_ref[tm:tm + halo, :]


def _ffn_up(hb, w_up, layer, conv_w, conv_b):
    s, d = hb.shape
    tm, tn = _tile(s, 1024), 512
    nj = D_FF // tn
    conv_b = conv_b.reshape(1, -1)
    return pl.pallas_call(
        _ffn_up_kernel,
        out_shape=jax.ShapeDtypeStruct((s, D_FF), BF16),
        grid=(nj, s // tm),
        in_specs=[pl.BlockSpec((tm, d), lambda j, i: (i, 0)),
                  pl.BlockSpec((None, d, tn), lambda j, i: (layer, 0, j)),
                  pl.BlockSpec((None, d, tn), lambda j, i: (layer, 0, j + nj)),
                  pl.BlockSpec((FFN_CONV, tn), lambda j, i: (0, j)),
                  pl.BlockSpec((FFN_CONV, tn), lambda j, i: (0, j + nj)),
                  pl.BlockSpec((1, tn), lambda j, i: (0, j)),
                  pl.BlockSpec((1, tn), lambda j, i: (0, j + nj))],
        out_specs=pl.BlockSpec((tm, tn), lambda j, i: (i, j)),
        scratch_shapes=[pltpu.VMEM((tm + SUBLANES, tn), F32), pltpu.VMEM((tm + SUBLANES, tn), F32),
                        pltpu.VMEM((d, tn), BF16), pltpu.VMEM((d, tn), BF16)],
        compiler_params=_params("parallel", "arbitrary"),
        name="ffn_up",
    )(hb, w_up, w_up, conv_w, conv_w, conv_b, conv_b)


def _ple_kernel(p_ref, hb_ref, wp_ref, wg_ref, o_ref, wpb_ref, wgb_ref):
    _cast_weights_once(wp_ref, wpb_ref)
    _cast_weights_once(wg_ref, wgb_ref)
    proj = jnp.dot(p_ref[...].astype(BF16), wpb_ref[...], preferred_element_type=F32)
    gate = jnp.dot(hb_ref[...], wgb_ref[...], preferred_element_type=F32)
    o_ref[...] = proj * jax.nn.sigmoid(gate)


def _ple(p, hb, w_proj, w_gate, layer):
    s, d = hb.shape
    tm, tn = _tile(s, 1024), 1024
    return pl.pallas_call(
        _ple_kernel,
        out_shape=jax.ShapeDtypeStruct((s, d), F32),
        grid=(d // tn, s // tm),
        in_specs=[pl.BlockSpec((None, None, tm, PE_DIM), lambda j, i: (layer, 0, i, 0)),
                  pl.BlockSpec((tm, d), lambda j, i: (i, 0)),
                  pl.BlockSpec((None, PE_DIM, tn), lambda j, i: (layer, 0, j)),
                  pl.BlockSpec((None, d, tn), lambda j, i: (layer, 0, j))],
        out_specs=pl.BlockSpec((tm, tn), lambda j, i: (i, j)),
        scratch_shapes=[pltpu.VMEM((PE_DIM, tn), BF16), pltpu.VMEM((d, tn), BF16)],
        compiler_params=_params("parallel", "arbitrary"),
        name="ple_gate",
    )(p, hb, w_proj, w_gate)


def _down_kernel(a_ref, w_ref, h_ref, ple_ref, g_ref, b_ref, o_ref, ob_ref):
    ffn = jnp.dot(a_ref[...], w_ref[...], preferred_element_type=F32)
    out = _layer_norm(DEEPNORM_ALPHA * h_ref[...] + ffn + ple_ref[...], g_ref[...], b_ref[...])
    o_ref[...] = out
    ob_ref[...] = out.astype(BF16)


def _down_ln(act, w_down, layer, h, ple, ln_g, ln_b):
    s, d = h.shape
    tm = _tile(s, 256)
    row = lambda width: pl.BlockSpec((1, width), lambda i: (0, 0))
    return pl.pallas_call(
        _down_kernel,
        out_shape=(jax.ShapeDtypeStruct((s, d), F32), jax.ShapeDtypeStruct((s, d), BF16)),
        grid=(s // tm,),
        in_specs=[pl.BlockSpec((tm, D_FF), lambda i: (i, 0)),
                  pl.BlockSpec((None, D_FF, d), lambda i: (layer, 0, 0), pipeline_mode=pl.Buffered(1)),
                  pl.BlockSpec((tm, d), lambda i: (i, 0)),
                  pl.BlockSpec((tm, d), lambda i: (i, 0)),
                  row(d), row(d)],
        out_specs=(pl.BlockSpec((tm, d), lambda i: (i, 0)), pl.BlockSpec((tm, d), lambda i: (i, 0))),
        compiler_params=_params("parallel"),
        name="down_ln",
    )(act, w_down, h, ple, ln_g.reshape(1, -1), ln_b.reshape(1, -1))


def kernel(x, p, positions, w_in, attn_norm_g, ssd_conv_w, ssd_conv_b, ssd_dt_bias, ssd_a_log, ssd_d, ssd_norm_g,
           w_out, ln1_g, ln1_b, w_up, ffn_conv_w, ffn_conv_b, w_down, w_pe_gate, w_pe_proj, ln2_g, ln2_b):
    b, s, d = x.shape
    assert b == 1 and d == D_MODEL
    cos, sin = _rope_tables(positions[0])
    h = x[0]
    hb = h.astype(BF16)
    dt_col = 3 * ATTN_WIDTH + ZX_WIDTH
    w_out_b = w_out.astype(BF16)
    w_down_b = w_down.astype(BF16)
    for i in range(w_in.shape[0]):
        qkv = _qkv_proj(hb, w_in, i, cos, sin)
        zx = _zx_proj(hb, w_in, i)
        attn = _moba_attention(qkv)
        ssd = _ssd(zx, hb, w_in[i, :, dt_col:], ssd_conv_w[i], ssd_conv_b[i], ssd_dt_bias[i], ssd_a_log[i],
                   ssd_d[i], ssd_norm_g[i])
        h1, h1b = _outproj_ln(attn, ssd, h, w_out_b, i, attn_norm_g[i], ln1_g[i], ln1_b[i])
        act = _ffn_up(h1b, w_up, i, ffn_conv_w[i], ffn_conv_b[i])
        ple = _ple(p, h1b, w_pe_proj, w_pe_gate, i)
        h, hb = _down_ln(act, w_down_b, i, h1, ple, ln2_g[i], ln2_b[i])
    return h[None]
```

```python
import functools

import jax
import jax.numpy as jnp
from jax import lax
from jax.experimental import pallas as pl
from jax.experimental.pallas import tpu as pltpu

F32 = jnp.float32
BF16 = jnp.bfloat16

D_MODEL = 2048
DEPTH = 4
PE_DIM = 256
ATTN_HEADS = 8
ATTN_HEAD_DIM = 128
ATTN_WIDTH = ATTN_HEADS * ATTN_HEAD_DIM
MOBA_BLOCK = 256
MOBA_TOPK = 3
ROPE_DIM = ATTN_HEAD_DIM // 4
ROPE_THETA = 500000.0
SSD_WIDTH = D_MODEL - ATTN_WIDTH
SSD_HEAD_DIM = 64
SSD_HEADS = SSD_WIDTH // SSD_HEAD_DIM
SSD_GROUPS = 2
SSD_STATE = 128
SSD_CONV = 4
SSD_CHUNK = 256
SSD_BC = 2 * SSD_GROUPS * SSD_STATE
SSD_CONV_DIM = SSD_WIDTH + SSD_BC
D_FF = 5632
FFN_CONV = 3
DEEPNORM_ALPHA = (2 * DEPTH) ** 0.25
LN_EPS = 1e-5
RMS_EPS = 1e-6

LANES = 128
SUBLANES = 8
VMEM_LIMIT = 56 * 1024 * 1024

NEG_BIG = -1e30
DT_PAD = LANES
ZX_WIDTH = SSD_WIDTH + SSD_CONV_DIM


def _params(*sem):
    return pltpu.CompilerParams(dimension_semantics=sem, vmem_limit_bytes=VMEM_LIMIT)


def _tile(n, pref):
    t = min(n, pref)
    assert n % t == 0, (n, t)
    return t


def _rope_table_kernel(pos_ref, invf_ref, sgn_ref, cos_ref, sin_ref):
    ang = pos_ref[...].astype(F32) * invf_ref[...]
    cos_ref[...] = jnp.cos(ang)
    sin_ref[...] = jnp.sin(ang) * sgn_ref[...]


def _rope_tables(positions):
    s = positions.shape[0]
    half = ROPE_DIM // 2
    inv_freq = 1.0 / (ROPE_THETA ** (jnp.arange(0, ROPE_DIM, 2, dtype=F32) / ROPE_DIM))
    zeros = jnp.zeros((LANES - ROPE_DIM,), F32)
    invf = jnp.concatenate([inv_freq, inv_freq, zeros]).reshape(1, LANES)
    sgn = jnp.concatenate([-jnp.ones((half,), F32), jnp.ones((half,), F32), zeros]).reshape(1, LANES)
    tm = _tile(s, 1024)
    return pl.pallas_call(
        _rope_table_kernel,
        out_shape=(jax.ShapeDtypeStruct((s, LANES), F32), jax.ShapeDtypeStruct((s, LANES), F32)),
        grid=(s // tm,),
        in_specs=[pl.BlockSpec((tm, 1), lambda i: (i, 0)),
                  pl.BlockSpec((1, LANES), lambda i: (0, 0)),
                  pl.BlockSpec((1, LANES), lambda i: (0, 0))],
        out_specs=(pl.BlockSpec((tm, LANES), lambda i: (i, 0)),
                   pl.BlockSpec((tm, LANES), lambda i: (i, 0))),
        compiler_params=_params("parallel"),
        name="rope_tables",
    )(positions.reshape(s, 1), invf, sgn)


PROJ_TN = 512


def _cast_weights_once(w_ref, wb_ref):
    @pl.when(pl.program_id(1) == 0)
    def _():
        wb_ref[...] = w_ref[...].astype(BF16)


def _qkv_kernel(x_ref, w_ref, cos_ref, sin_ref, o_ref, *, rope_tiles):
    acc = jnp.dot(x_ref[...], w_ref[...], preferred_element_type=F32)
    j = pl.program_id(0)

    @pl.when(j < rope_tiles)
    def _():
        cos = cos_ref[...]
        sin = sin_ref[...]
        lane = lax.broadcasted_iota(jnp.int32, cos.shape, 1)
        half = ROPE_DIM // 2
        for c in range(acc.shape[1] // LANES):
            t = acc[:, c * LANES:(c + 1) * LANES]
            partner = jnp.where(lane < half, pltpu.roll(t, LANES - half, 1), pltpu.roll(t, half, 1))
            o_ref[:, c * LANES:(c + 1) * LANES] = (t * cos + partner * sin).astype(o_ref.dtype)

    @pl.when(j >= rope_tiles)
    def _():
        o_ref[...] = acc.astype(o_ref.dtype)


def _qkv_proj(xb, w_in, layer, cos, sin):
    s, k = xb.shape
    n = 3 * ATTN_WIDTH
    tm, tn = _tile(s, 2048), PROJ_TN
    return pl.pallas_call(
        functools.partial(_qkv_kernel, rope_tiles=2 * ATTN_WIDTH // tn),
        out_shape=jax.ShapeDtypeStruct((s, n), BF16),
        grid=(n // tn, s // tm),
        in_specs=[pl.BlockSpec((tm, k), lambda j, i: (i, 0)),
                  pl.BlockSpec((None, k, tn), lambda j, i: (layer, 0, j)),
                  pl.BlockSpec((tm, LANES), lambda j, i: (i, 0)),
                  pl.BlockSpec((tm, LANES), lambda j, i: (i, 0))],
        out_specs=pl.BlockSpec((tm, tn), lambda j, i: (i, j)),
        compiler_params=_params("parallel", "arbitrary"),
        name="qkv_proj",
    )(xb, w_in, cos, sin)


def _zx_kernel(x_ref, w_ref, o_ref):
    o_ref[...] = jnp.dot(x_ref[...], w_ref[...], preferred_element_type=F32)


def _zx_proj(xb, w_in, layer):
    s, k = xb.shape
    n = ZX_WIDTH
    tm, tn = _tile(s, 2048), PROJ_TN
    first = 3 * ATTN_WIDTH // tn
    return pl.pallas_call(
        _zx_kernel,
        out_shape=jax.ShapeDtypeStruct((s, n), F32),
        grid=(n // tn, s // tm),
        in_specs=[pl.BlockSpec((tm, k), lambda j, i: (i, 0)),
                  pl.BlockSpec((None, k, tn), lambda j, i: (layer, 0, first + j))],
        out_specs=pl.BlockSpec((tm, tn), lambda j, i: (i, j)),
        compiler_params=_params("parallel", "arbitrary"),
        name="zx_proj",
    )(xb, w_in)


MOBA_GROUP = 2
MOBA_HEADS_PER_STEP = 2
ONES_ROWS = 16
LOG2_E = 1.4426950408889634


def _moba_kernel(q_ref, k_ref, v_ref, o_ref, kmean_ref, vt_ref, sel_ref, s_ref, acc_ref, *, nb):
    bs = MOBA_BLOCK
    dh = ATTN_HEAD_DIM
    grp = MOBA_GROUP
    heads = MOBA_HEADS_PER_STEP
    qi = pl.program_id(1)
    c = dh ** -0.5 * LOG2_E
    ones = jnp.ones((ONES_ROWS, bs), F32)

    def v_transposed(rows, cols):
        return v_ref[rows, cols].astype(F32).T

    @pl.when(qi == 0)
    def _():
        kmean_ref[...] = jnp.zeros_like(kmean_ref)
        vt_ref[:, :, dh:, :] = jnp.ones((heads, nb // grp, ONES_ROWS, grp * bs), BF16)

        def body(t, carry):
            for hh in range(heads):
                cols = slice(hh * dh, (hh + 1) * dh)
                for u in range(grp):
                    n = t * grp + u
                    rows = pl.ds(pl.multiple_of(n * bs, bs), bs)
                    kmean_ref[hh, pl.ds(n, 1), :] = jnp.mean(k_ref[rows, cols].astype(F32), axis=0, keepdims=True)
                    vt_ref[hh, t, 0:dh, u * bs:(u + 1) * bs] = v_transposed(rows, cols).astype(BF16)
            return carry

        lax.fori_loop(0, nb // grp, body, 0)

    own_rows = pl.ds(pl.multiple_of(qi * bs, bs), bs)
    q_ts, m_init = [], []
    for hh in range(heads):
        cols = slice(hh * dh, (hh + 1) * dh)
        q_t = q_ref[:, cols].astype(F32).T.astype(BF16)
        q_ts.append(q_t)

        gate = jnp.dot(kmean_ref[hh].astype(BF16), q_t, preferred_element_type=F32)
        blk = lax.broadcasted_iota(jnp.int32, gate.shape, 0)
        g = jnp.where(blk < qi, gate, -jnp.inf)
        sel = jnp.zeros(gate.shape, F32)
        for _ in range(MOBA_TOPK):
            mx = jnp.max(g, axis=0, keepdims=True)
            first = jnp.min(jnp.where(g == mx, blk, nb), axis=0, keepdims=True)
            pick = blk == jnp.where(mx > -jnp.inf, first, -1)
            sel = jnp.where(pick, 1.0, sel)
            g = jnp.where(pick, -jnp.inf, g)
        sel_ref[hh] = jnp.concatenate([sel, jnp.zeros((SUBLANES, bs), F32)], axis=0)

        s = jnp.dot(k_ref[own_rows, cols], q_t, preferred_element_type=F32)
        row = lax.broadcasted_iota(jnp.int32, s.shape, 0)
        col = lax.broadcasted_iota(jnp.int32, s.shape, 1)
        s = jnp.where(row <= col, s, NEG_BIG)
        m0 = jnp.max(s, axis=0, keepdims=True)
        p = jnp.exp2((s - m0) * c)
        vt_own = jnp.concatenate([v_transposed(own_rows, cols), ones], axis=0).astype(BF16)
        acc_ref[hh] = jnp.dot(vt_own, p.astype(BF16), preferred_element_type=F32)
        m_init.append(m0)

    last_group = nb // grp - 1

    def scores_into(slot, t):
        tt = jnp.minimum(t, last_group)
        keys = pl.ds(pl.multiple_of(tt * (grp * bs), grp * bs), grp * bs)
        for hh in range(heads):
            s_ref[slot, hh] = jnp.dot(k_ref[keys, hh * dh:(hh + 1) * dh], q_ts[hh], preferred_element_type=F32)

    def softmax_pv(slot, t, ms):
        probs, stats = [], []
        for hh in range(heads):
            parts = []
            for u in range(grp):
                picked = sel_ref[hh, pl.ds(t * grp + u, 1), :] > 0.0
                parts.append(jnp.where(picked, s_ref[slot, hh, u * bs:(u + 1) * bs, :], NEG_BIG))
            mx = parts[0]
            for u in range(1, grp):
                mx = jnp.maximum(mx, parts[u])
            m_new = jnp.maximum(ms[hh], jnp.max(mx, axis=0, keepdims=True))
            stats.append((m_new, jnp.exp2((ms[hh] - m_new) * c)))
            probs.append(jnp.concatenate([jnp.exp2((part - m_new) * c).astype(BF16) for part in parts], axis=0))
        tt = jnp.minimum(t, last_group)
        for hh in range(heads):
            pv = jnp.dot(vt_ref[hh, tt], probs[hh], preferred_element_type=F32)
            acc_ref[hh] = stats[hh][1] * acc_ref[hh] + pv
        return tuple(m_new for m_new, _ in stats)

    def body(d, ms):
        t = 2 * d
        scores_into(1, t + 1)
        ms = softmax_pv(0, t, ms)
        scores_into(0, t + 2)
        return softmax_pv(1, t + 1, ms)

    trips = (qi + grp - 1) // grp
    scores_into(0, 0)
    lax.fori_loop(0, (trips + 1) // 2, body, tuple(m_init))
    for hh in range(heads):
        acc = acc_ref[hh]
        o_ref[:, hh * dh:(hh + 1) * dh] = (acc[0:dh] / acc[dh:dh + 1]).T.astype(o_ref.dtype)


def _moba_attention(qkv):
    s = qkv.shape[0]
    bs = MOBA_BLOCK
    nb = s // bs
    heads = MOBA_HEADS_PER_STEP
    assert s % bs == 0 and nb % MOBA_GROUP == 0 and ATTN_HEADS % heads == 0
    nbp = -(-nb // SUBLANES) * SUBLANES
    width = heads * ATTN_HEAD_DIM
    steps = ATTN_HEADS // heads
    return pl.pallas_call(
        functools.partial(_moba_kernel, nb=nb),
        out_shape=jax.ShapeDtypeStruct((s, ATTN_WIDTH), F32),
        grid=(steps, nb),
        in_specs=[pl.BlockSpec((bs, width), lambda h, i: (i, h)),
                  pl.BlockSpec((s, width), lambda h, i: (0, steps + h)),
                  pl.BlockSpec((s, width), lambda h, i: (0, 2 * steps + h))],
        out_specs=pl.BlockSpec((bs, width), lambda h, i: (i, h)),
        scratch_shapes=[pltpu.VMEM((heads, nbp, ATTN_HEAD_DIM), F32),
                        pltpu.VMEM((heads, nb // MOBA_GROUP, ATTN_HEAD_DIM + ONES_ROWS, MOBA_GROUP * bs), BF16),
                        pltpu.VMEM((heads, nbp + SUBLANES, bs), F32),
                        pltpu.VMEM((2, heads, MOBA_GROUP * bs, bs), F32),
                        pltpu.VMEM((heads, ATTN_HEAD_DIM + ONES_ROWS, bs), F32)],
        compiler_params=_params("parallel", "arbitrary"),
        name="moba_attention",
    )(qkv, qkv, qkv)


def _silu(x):
    return x * jax.nn.sigmoid(x)


def _ssd_kernel(z_ref, xs_ref, bc_ref, hb_ref, wdt_ref, cwx_ref, cwbc_ref, cbx_ref, cbbc_ref, dtb_ref, alog_ref,
                dskip_ref, ng_ref, expand_ref, o_ref, xpad_ref, bcpad_ref, state_ref, y_ref):
    L = SSD_CHUNK
    P = SSD_HEAD_DIM
    N = SSD_STATE
    halo = SUBLANES
    c = pl.program_id(0)
    hi = lax.Precision.HIGHEST

    @pl.when(c == 0)
    def _():
        xpad_ref[0:halo, :] = jnp.zeros((halo, xpad_ref.shape[1]), F32)
        bcpad_ref[0:halo, :] = jnp.zeros((halo, bcpad_ref.shape[1]), F32)
        state_ref[...] = jnp.zeros_like(state_ref)

    xpad_ref[halo:halo + L, :] = xs_ref[...]
    bcpad_ref[halo:halo + L, :] = bc_ref[...]

    def conv(pad_ref, w_ref, b_ref):
        acc = b_ref[...]
        for k in range(SSD_CONV):
            off = halo - (SSD_CONV - 1) + k
            acc = acc + pad_ref[off:off + L, :] * w_ref[k:k + 1, :]
        return _silu(acc)

    xs = conv(xpad_ref, cwx_ref, cbx_ref)
    bc = conv(bcpad_ref, cwbc_ref, cbbc_ref)
    xpad_ref[0:halo, :] = xpad_ref[L:L + halo, :]
    bcpad_ref[0:halo, :] = bcpad_ref[L:L + halo, :]

    lane = lax.broadcasted_iota(jnp.int32, (1, LANES), 1)
    dtr = jnp.dot(hb_ref[...], wdt_ref[...], preferred_element_type=F32) + dtb_ref[...]
    dt = jnp.maximum(dtr, 0.0) + jnp.log1p(jnp.exp(-jnp.abs(dtr)))
    a = jnp.where(lane < SSD_HEADS, -jnp.exp(alog_ref[...]), 0.0)
    da = dt * a
    row = lax.broadcasted_iota(jnp.int32, (L, L), 0)
    col = lax.broadcasted_iota(jnp.int32, (L, L), 1)
    tril = row >= col
    a_cs = jnp.dot(tril.astype(F32), da, preferred_element_type=F32, precision=hi)
    a_cs_t = a_cs.T
    exp_a = jnp.exp(a_cs)
    decay = jnp.exp(a_cs[L - 1:L, :] - a_cs)
    stacked = jnp.concatenate([dt, exp_a, decay], axis=0)
    wide = jnp.dot(stacked, expand_ref[...], preferred_element_type=F32, precision=hi)
    dt_w, exp_w, dec_w = wide[0:L], wide[L:2 * L], wide[2 * L:3 * L]
    xdt = xs * dt_w
    xdec_b = (xdt * dec_w).astype(BF16)
    lane_w = lax.broadcasted_iota(jnp.int32, (L, LANES), 1)
    first_head = lane_w < P

    gw = SSD_WIDTH // SSD_GROUPS
    pairs_per_group = gw // LANES
    for g in range(SSD_GROUPS):
        b_g = bc[:, g * N:(g + 1) * N]
        c_g = bc[:, SSD_GROUPS * N + g * N:SSD_GROUPS * N + (g + 1) * N].astype(BF16)
        b_t = b_g.T.astype(BF16)
        cb = jnp.dot(c_g, b_t, preferred_element_type=F32)
        ssq = jnp.zeros((L, 1), F32)
        for i in range(pairs_per_group):
            pi = g * pairs_per_group + i
            blk = slice(pi * LANES, (pi + 1) * LANES)
            x_blk = xdt[:, blk]
            y_pair = jnp.zeros((L, LANES), F32)
            for e in range(2):
                h = 2 * pi + e
                seg = jnp.exp(jnp.where(tril, a_cs[:, h:h + 1] - a_cs_t[h:h + 1, :], -jnp.inf))
                mat = (cb * seg).astype(BF16)
                keep = first_head if e == 0 else jnp.logical_not(first_head)
                y_pair = y_pair + jnp.dot(mat, jnp.where(keep, x_blk, 0.0).astype(BF16),
                                          preferred_element_type=F32)
            st = jnp.dot(b_t, xdec_b[:, blk], preferred_element_type=F32)
            h_prev = state_ref[pi]
            y_off = jnp.dot(c_g, h_prev.astype(BF16), preferred_element_type=F32) * exp_w[:, blk]
            state_ref[pi] = h_prev * exp_w[L - 1:L, blk] + st
            y = y_pair + y_off + xs[:, blk] * dskip_ref[:, blk]
            yg = y * _silu(z_ref[:, blk])
            ssq = ssq + jnp.sum(yg * yg, axis=-1, keepdims=True)
            y_ref[:, blk] = yg
        rs = lax.rsqrt(ssq / gw + RMS_EPS)
        gsl = slice(g * gw, (g + 1) * gw)
        o_ref[:, gsl] = (y_ref[:, gsl] * rs * ng_ref[:, gsl]).astype(o_ref.dtype)


def _pad_lanes(v, width=LANES):
    return jnp.pad(v.astype(F32), (0, width - v.shape[0])).reshape(1, width)


def _ssd(zx, hb, w_dt, conv_w, conv_b, dt_bias, a_log, d_skip, norm_g):
    s = zx.shape[0]
    d = hb.shape[1]
    w_dt = jnp.pad(w_dt, ((0, 0), (0, DT_PAD - w_dt.shape[1]))).astype(BF16)
    L = SSD_CHUNK
    assert s % L == 0
    w = SSD_WIDTH
    expand = (jnp.arange(LANES)[:, None] == (jnp.arange(w)[None, :] // SSD_HEAD_DIM)).astype(F32)
    d_wide = jnp.repeat(d_skip.astype(F32), SSD_HEAD_DIM).reshape(1, w)
    row1 = lambda width: pl.BlockSpec((1, width), lambda c: (0, 0))
    return pl.pallas_call(
        _ssd_kernel,
        out_shape=jax.ShapeDtypeStruct((s, w), BF16),
        grid=(s // L,),
        in_specs=[pl.BlockSpec((L, w), lambda c: (c, 0)),
                  pl.BlockSpec((L, w), lambda c: (c, 1)),
                  pl.BlockSpec((L, SSD_BC), lambda c: (c, 2 * w // SSD_BC)),
                  pl.BlockSpec((L, d), lambda c: (c, 0)),
                  pl.BlockSpec((d, DT_PAD), lambda c: (0, 0)),
                  pl.BlockSpec((SSD_CONV, w), lambda c: (0, 0)),
                  pl.BlockSpec((SSD_CONV, SSD_BC), lambda c: (0, 0)),
                  row1(w), row1(SSD_BC), row1(LANES), row1(LANES), row1(w), row1(w),
                  pl.BlockSpec((LANES, w), lambda c: (0, 0))],
        out_specs=pl.BlockSpec((L, w), lambda c: (c, 0)),
        scratch_shapes=[pltpu.VMEM((L + SUBLANES, w), F32),
                        pltpu.VMEM((L + SUBLANES, SSD_BC), F32),
                        pltpu.VMEM((w // LANES, SSD_STATE, LANES), F32),
                        pltpu.VMEM((L, w), F32)],
        compiler_params=_params("arbitrary"),
        name="ssd_scan",
    )(zx, zx, zx, hb, w_dt, conv_w[:, :w], conv_w[:, w:], conv_b[:w].reshape(1, w), conv_b[w:].reshape(1, SSD_BC),
      _pad_lanes(dt_bias), _pad_lanes(a_log), d_wide, norm_g.reshape(1, w), expand)


def _layer_norm(y, g, b):
    mu = jnp.mean(y, axis=-1, keepdims=True)
    d = y - mu
    var = jnp.mean(d * d, axis=-1, keepdims=True)
    return d * lax.rsqrt(var + LN_EPS) * g + b


def _outproj_kernel(attn_ref, ssd_ref, h_ref, w_ref, ag_ref, g_ref, b_ref, o_ref, ob_ref):
    attn = attn_ref[...]
    attn_n = attn * lax.rsqrt(jnp.mean(attn * attn, axis=-1, keepdims=True) + RMS_EPS) * ag_ref[...]
    aw = ATTN_WIDTH
    mix = jnp.dot(attn_n.astype(BF16), w_ref[0:aw, :], preferred_element_type=F32)
    mix = mix + jnp.dot(ssd_ref[...], w_ref[aw:, :], preferred_element_type=F32)
    out = _layer_norm(DEEPNORM_ALPHA * h_ref[...] + mix, g_ref[...], b_ref[...])
    o_ref[...] = out
    ob_ref[...] = out.astype(BF16)


def _outproj_ln(attn, ssd, h, w, layer, attn_g, ln_g, ln_b):
    s, d = h.shape
    tm = _tile(s, 512)
    row = lambda width: pl.BlockSpec((1, width), lambda i: (0, 0))
    return pl.pallas_call(
        _outproj_kernel,
        out_shape=(jax.ShapeDtypeStruct((s, d), F32), jax.ShapeDtypeStruct((s, d), BF16)),
        grid=(s // tm,),
        in_specs=[pl.BlockSpec((tm, ATTN_WIDTH), lambda i: (i, 0)),
                  pl.BlockSpec((tm, SSD_WIDTH), lambda i: (i, 0)),
                  pl.BlockSpec((tm, d), lambda i: (i, 0)),
                  pl.BlockSpec((None, d, d), lambda i: (layer, 0, 0), pipeline_mode=pl.Buffered(1)),
                  row(ATTN_WIDTH), row(d), row(d)],
        out_specs=(pl.BlockSpec((tm, d), lambda i: (i, 0)), pl.BlockSpec((tm, d), lambda i: (i, 0))),
        compiler_params=_params("parallel"),
        name="outproj_ln",
    )(attn, ssd, h, w, attn_g.reshape(1, -1), ln_g.reshape(1, -1), ln_b.reshape(1, -1))


def _ffn_up_kernel(x_ref, wg_ref, wu_ref, cwg_ref, cwu_ref, cbg_ref, cbu_ref, o_ref, gpad_ref, upad_ref,
                   wgb_ref, wub_ref):
    tm = x_ref.shape[0]
    halo = SUBLANES
    _cast_weights_once(wg_ref, wgb_ref)
    _cast_weights_once(wu_ref, wub_ref)

    @pl.when(pl.program_id(1) == 0)
    def _():
        gpad_ref[0:halo, :] = jnp.zeros((halo, gpad_ref.shape[1]), F32)
        upad_ref[0:halo, :] = jnp.zeros((halo, upad_ref.shape[1]), F32)

    x = x_ref[...]
    gpad_ref[halo:halo + tm, :] = jnp.dot(x, wgb_ref[...], preferred_element_type=F32)
    upad_ref[halo:halo + tm, :] = jnp.dot(x, wub_ref[...], preferred_element_type=F32)

    def conv(pad_ref, w_ref, b_ref):
        acc = b_ref[...]
        for k in range(FFN_CONV):
            off = halo - (FFN_CONV - 1) + k
            acc = acc + pad_ref[off:off + tm, :] * w_ref[k:k + 1, :]
        return acc

    g = conv(gpad_ref, cwg_ref, cbg_ref)
    u = conv(upad_ref, cwu_ref, cbu_ref)
    o_ref[...] = (_silu(g) * u).astype(o_ref.dtype)
    gpad_ref[0:halo, :] = gpad_ref[tm:tm + halo, :]
    upad_ref[0:halo, :] = upad_ref[tm:tm + halo, :]


def _ffn_up(hb, w_up, layer, conv_w, conv_b):
    s, d = hb.shape
    tm, tn = _tile(s, 1024), 512
    nj = D_FF // tn
    conv_b = conv_b.reshape(1, -1)
    return pl.pallas_call(
        _ffn_up_kernel,
        out_shape=jax.ShapeDtypeStruct((s, D_FF), BF16),
        grid=(nj, s // tm),
        in_specs=[pl.BlockSpec((tm, d), lambda j, i: (i, 0)),
                  pl.BlockSpec((None, d, tn), lambda j, i: (layer, 0, j)),
                  pl.BlockSpec((None, d, tn), lambda j, i: (layer, 0, j + nj)),
                  pl.BlockSpec((FFN_CONV, tn), lambda j, i: (0, j)),
                  pl.BlockSpec((FFN_CONV, tn), lambda j, i: (0, j + nj)),
                  pl.BlockSpec((1, tn), lambda j, i: (0, j)),
                  pl.BlockSpec((1, tn), lambda j, i: (0, j + nj))],
        out_specs=pl.BlockSpec((tm, tn), lambda j, i: (i, j)),
        scratch_shapes=[pltpu.VMEM((tm + SUBLANES, tn), F32), pltpu.VMEM((tm + SUBLANES, tn), F32),
                        pltpu.VMEM((d, tn), BF16), pltpu.VMEM((d, tn), BF16)],
        compiler_params=_params("parallel", "arbitrary"),
        name="ffn_up",
    )(hb, w_up, w_up, conv_w, conv_w, conv_b, conv_b)


def _ple_kernel(p_ref, hb_ref, wp_ref, wg_ref, o_ref, wpb_ref, wgb_ref):
    _cast_weights_once(wp_ref, wpb_ref)
    _cast_weights_once(wg_ref, wgb_ref)
    proj = jnp.dot(p_ref[...].astype(BF16), wpb_ref[...], preferred_element_type=F32)
    gate = jnp.dot(hb_ref[...], wgb_ref[...], preferred_element_type=F32)
    o_ref[...] = proj * jax.nn.sigmoid(gate)


def _ple(p, hb, w_proj, w_gate, layer):
    s, d = hb.shape
    tm, tn = _tile(s, 1024), 1024
    return pl.pallas_call(
        _ple_kernel,
        out_shape=jax.ShapeDtypeStruct((s, d), F32),
        grid=(d // tn, s // tm),
        in_specs=[pl.BlockSpec((None, None, tm, PE_DIM), lambda j, i: (layer, 0, i, 0)),
                  pl.BlockSpec((tm, d), lambda j, i: (i, 0)),
                  pl.BlockSpec((None, PE_DIM, tn), lambda j, i: (layer, 0, j)),
                  pl.BlockSpec((None, d, tn), lambda j, i: (layer, 0, j))],
        out_specs=pl.BlockSpec((tm, tn), lambda j, i: (i, j)),
        scratch_shapes=[pltpu.VMEM((PE_DIM, tn), BF16), pltpu.VMEM((d, tn), BF16)],
        compiler_params=_params("parallel", "arbitrary"),
        name="ple_gate",
    )(p, hb, w_proj, w_gate)


def _down_kernel(a_ref, w_ref, h_ref, ple_ref, g_ref, b_ref, o_ref, ob_ref):
    ffn = jnp.dot(a_ref[...], w_ref[...], preferred_element_type=F32)
    out = _layer_norm(DEEPNORM_ALPHA * h_ref[...] + ffn + ple_ref[...], g_ref[...], b_ref[...])
    o_ref[...] = out
    ob_ref[...] = out.astype(BF16)


def _down_ln(act, w_down, layer, h, ple, ln_g, ln_b):
    s, d = h.shape
    tm = _tile(s, 256)
    row = lambda width: pl.BlockSpec((1, width), lambda i: (0, 0))
    return pl.pallas_call(
        _down_kernel,
        out_shape=(jax.ShapeDtypeStruct((s, d), F32), jax.ShapeDtypeStruct((s, d), BF16)),
        grid=(s // tm,),
        in_specs=[pl.BlockSpec((tm, D_FF), lambda i: (i, 0)),
                  pl.BlockSpec((None, D_FF, d), lambda i: (layer, 0, 0), pipeline_mode=pl.Buffered(1)),
                  pl.BlockSpec((tm, d), lambda i: (i, 0)),
                  pl.BlockSpec((tm, d), lambda i: (i, 0)),
                  row(d), row(d)],
        out_specs=(pl.BlockSpec((tm, d), lambda i: (i, 0)), pl.BlockSpec((tm, d), lambda i: (i, 0))),
        compiler_params=_params("parallel"),
        name="down_ln",
    )(act, w_down, h, ple, ln_g.reshape(1, -1), ln_b.reshape(1, -1))


def kernel(x, p, positions, w_in, attn_norm_g, ssd_conv_w, ssd_conv_b, ssd_dt_bias, ssd_a_log, ssd_d, ssd_norm_g,
           w_out, ln1_g, ln1_b, w_up, ffn_conv_w, ffn_conv_b, w_down, w_pe_gate, w_pe_proj, ln2_g, ln2_b):
    b, s, d = x.shape
    assert b == 1 and d == D_MODEL
    cos, sin = _rope_tables(positions[0])
    h = x[0]
    hb = h.astype(BF16)
    dt_col = 3 * ATTN_WIDTH + ZX_WIDTH
    w_in_b = w_in[:, :, :dt_col].astype(BF16)
    w_out_b = w_out.astype(BF16)
    w_down_b = w_down.astype(BF16)
    for i in range(w_in.shape[0]):
        qkv = _qkv_proj(hb, w_in_b, i, cos, sin)
        zx = _zx_proj(hb, w_in_b, i)
        attn = _moba_attention(qkv)
        ssd = _ssd(zx, hb, w_in[i, :, dt_col:], ssd_conv_w[i], ssd_conv_b[i], ssd_dt_bias[i], ssd_a_log[i],
                   ssd_d[i], ssd_norm_g[i])
        h1, h1b = _outproj_ln(attn, ssd, h, w_out_b, i, attn_norm_g[i], ln1_g[i], ln1_b[i])
        act = _ffn_up(h1b, w_up, i, ffn_conv_w[i], ffn_conv_b[i])
        ple = _ple(p, h1b, w_pe_proj, w_pe_gate, i)
        h, hb = _down_ln(act, w_down_b, i, h1, ple, ln2_g[i], ln2_b[i])
    return h[None]
```

```python
import functools

import jax
import jax.numpy as jnp
from jax import lax
from jax.experimental import pallas as pl
from jax.experimental.pallas import tpu as pltpu

F32 = jnp.float32
BF16 = jnp.bfloat16

D_MODEL = 2048
DEPTH = 4
PE_DIM = 256
ATTN_HEADS = 8
ATTN_HEAD_DIM = 128
ATTN_WIDTH = ATTN_HEADS * ATTN_HEAD_DIM
MOBA_BLOCK = 256
MOBA_TOPK = 3
ROPE_DIM = ATTN_HEAD_DIM // 4
ROPE_THETA = 500000.0
SSD_WIDTH = D_MODEL - ATTN_WIDTH
SSD_HEAD_DIM = 64
SSD_HEADS = SSD_WIDTH // SSD_HEAD_DIM
SSD_GROUPS = 2
SSD_STATE = 128
SSD_CONV = 4
SSD_CHUNK = 256
SSD_BC = 2 * SSD_GROUPS * SSD_STATE
SSD_CONV_DIM = SSD_WIDTH + SSD_BC
D_FF = 5632
FFN_CONV = 3
DEEPNORM_ALPHA = (2 * DEPTH) ** 0.25
LN_EPS = 1e-5
RMS_EPS = 1e-6

LANES = 128
SUBLANES = 8
VMEM_LIMIT = 56 * 1024 * 1024

NEG_BIG = -1e30
DT_PAD = LANES
ZX_WIDTH = SSD_WIDTH + SSD_CONV_DIM


def _params(*sem):
    return pltpu.CompilerParams(dimension_semantics=sem, vmem_limit_bytes=VMEM_LIMIT)


def _tile(n, pref):
    t = min(n, pref)
    assert n % t == 0, (n, t)
    return t


def _rope_table_kernel(pos_ref, invf_ref, sgn_ref, cos_ref, sin_ref):
    ang = pos_ref[...].astype(F32) * invf_ref[...]
    cos_ref[...] = jnp.cos(ang)
    sin_ref[...] = jnp.sin(ang) * sgn_ref[...]


def _rope_tables(positions):
    s = positions.shape[0]
    half = ROPE_DIM // 2
    inv_freq = 1.0 / (ROPE_THETA ** (jnp.arange(0, ROPE_DIM, 2, dtype=F32) / ROPE_DIM))
    zeros = jnp.zeros((LANES - ROPE_DIM,), F32)
    invf = jnp.concatenate([inv_freq, inv_freq, zeros]).reshape(1, LANES)
    sgn = jnp.concatenate([-jnp.ones((half,), F32), jnp.ones((half,), F32), zeros]).reshape(1, LANES)
    tm = _tile(s, 1024)
    return pl.pallas_call(
        _rope_table_kernel,
        out_shape=(jax.ShapeDtypeStruct((s, LANES), F32), jax.ShapeDtypeStruct((s, LANES), F32)),
        grid=(s // tm,),
        in_specs=[pl.BlockSpec((tm, 1), lambda i: (i, 0)),
                  pl.BlockSpec((1, LANES), lambda i: (0, 0)),
                  pl.BlockSpec((1, LANES), lambda i: (0, 0))],
        out_specs=(pl.BlockSpec((tm, LANES), lambda i: (i, 0)),
                   pl.BlockSpec((tm, LANES), lambda i: (i, 0))),
        compiler_params=_params("parallel"),
        name="rope_tables",
    )(positions.reshape(s, 1), invf, sgn)


PROJ_TN = 512


def _cast_weights_once(w_ref, wb_ref):
    @pl.when(pl.program_id(1) == 0)
    def _():
        wb_ref[...] = w_ref[...].astype(BF16)


def _dot_wt(x, wt):
    return lax.dot_general(x, wt, (((1,), (1,)), ((), ())), preferred_element_type=F32)


def _qkv_kernel(x_ref, wt_ref, cos_ref, sin_ref, o_ref, wb_ref, *, rope_tiles):
    _cast_weights_once(wt_ref, wb_ref)
    acc = _dot_wt(x_ref[...], wb_ref[...])
    j = pl.program_id(0)

    @pl.when(j < rope_tiles)
    def _():
        cos = cos_ref[...]
        sin = sin_ref[...]
        lane = lax.broadcasted_iota(jnp.int32, cos.shape, 1)
        half = ROPE_DIM // 2
        for c in range(acc.shape[1] // LANES):
            t = acc[:, c * LANES:(c + 1) * LANES]
            partner = jnp.where(lane < half, pltpu.roll(t, LANES - half, 1), pltpu.roll(t, half, 1))
            o_ref[:, c * LANES:(c + 1) * LANES] = (t * cos + partner * sin).astype(o_ref.dtype)

    @pl.when(j >= rope_tiles)
    def _():
        o_ref[...] = acc.astype(o_ref.dtype)


def _qkv_proj(xb, w_in_t, layer, cos, sin):
    s, k = xb.shape
    n = 3 * ATTN_WIDTH
    tm, tn = _tile(s, 2048), PROJ_TN
    return pl.pallas_call(
        functools.partial(_qkv_kernel, rope_tiles=2 * ATTN_WIDTH // tn),
        out_shape=jax.ShapeDtypeStruct((s, n), BF16),
        grid=(n // tn, s // tm),
        in_specs=[pl.BlockSpec((tm, k), lambda j, i: (i, 0)),
                  pl.BlockSpec((None, tn, k), lambda j, i: (layer, j, 0)),
                  pl.BlockSpec((tm, LANES), lambda j, i: (i, 0)),
                  pl.BlockSpec((tm, LANES), lambda j, i: (i, 0))],
        out_specs=pl.BlockSpec((tm, tn), lambda j, i: (i, j)),
        scratch_shapes=[pltpu.VMEM((tn, k), BF16)],
        compiler_params=_params("parallel", "arbitrary"),
        name="qkv_proj",
    )(xb, w_in_t, cos, sin)


def _zx_kernel(x_ref, wt_ref, o_ref, wb_ref):
    _cast_weights_once(wt_ref, wb_ref)
    o_ref[...] = _dot_wt(x_ref[...], wb_ref[...])


def _zx_proj(xb, w_in_t, layer):
    s, k = xb.shape
    n = ZX_WIDTH
    tm, tn = _tile(s, 2048), PROJ_TN
    first = 3 * ATTN_WIDTH // tn
    return pl.pallas_call(
        _zx_kernel,
        out_shape=jax.ShapeDtypeStruct((s, n), F32),
        grid=(n // tn, s // tm),
        in_specs=[pl.BlockSpec((tm, k), lambda j, i: (i, 0)),
                  pl.BlockSpec((None, tn, k), lambda j, i: (layer, first + j, 0))],
        out_specs=pl.BlockSpec((tm, tn), lambda j, i: (i, j)),
        scratch_shapes=[pltpu.VMEM((tn, k), BF16)],
        compiler_params=_params("parallel", "arbitrary"),
        name="zx_proj",
    )(xb, w_in_t)


MOBA_GROUP = 2
MOBA_HEADS_PER_STEP = 2
ONES_ROWS = 16
LOG2_E = 1.4426950408889634


def _moba_kernel(q_ref, k_ref, v_ref, o_ref, kmean_ref, vt_ref, sel_ref, s_ref, acc_ref, *, nb):
    bs = MOBA_BLOCK
    dh = ATTN_HEAD_DIM
    grp = MOBA_GROUP
    heads = MOBA_HEADS_PER_STEP
    qi = pl.program_id(1)
    c = dh ** -0.5 * LOG2_E
    ones = jnp.ones((ONES_ROWS, bs), F32)

    def v_transposed(rows, cols):
        return v_ref[rows, cols].astype(F32).T

    @pl.when(qi == 0)
    def _():
        kmean_ref[...] = jnp.zeros_like(kmean_ref)
        vt_ref[:, :, dh:, :] = jnp.ones((heads, nb // grp, ONES_ROWS, grp * bs), BF16)

        def body(t, carry):
            for hh in range(heads):
                cols = slice(hh * dh, (hh + 1) * dh)
                for u in range(grp):
                    n = t * grp + u
                    rows = pl.ds(pl.multiple_of(n * bs, bs), bs)
                    kmean_ref[hh, pl.ds(n, 1), :] = jnp.mean(k_ref[rows, cols].astype(F32), axis=0, keepdims=True)
                    vt_ref[hh, t, 0:dh, u * bs:(u + 1) * bs] = v_transposed(rows, cols).astype(BF16)
            return carry

        lax.fori_loop(0, nb // grp, body, 0)

    own_rows = pl.ds(pl.multiple_of(qi * bs, bs), bs)
    q_ts, m_init = [], []
    for hh in range(heads):
        cols = slice(hh * dh, (hh + 1) * dh)
        q_t = q_ref[:, cols].astype(F32).T.astype(BF16)
        q_ts.append(q_t)

        gate = jnp.dot(kmean_ref[hh].astype(BF16), q_t, preferred_element_type=F32)
        blk = lax.broadcasted_iota(jnp.int32, gate.shape, 0)
        g = jnp.where(blk < qi, gate, -jnp.inf)
        sel = jnp.zeros(gate.shape, F32)
        for _ in range(MOBA_TOPK):
            mx = jnp.max(g, axis=0, keepdims=True)
            first = jnp.min(jnp.where(g == mx, blk, nb), axis=0, keepdims=True)
            pick = blk == jnp.where(mx > -jnp.inf, first, -1)
            sel = jnp.where(pick, 1.0, sel)
            g = jnp.where(pick, -jnp.inf, g)
        sel_ref[hh] = jnp.concatenate([sel, jnp.zeros((SUBLANES, bs), F32)], axis=0)

        s = jnp.dot(k_ref[own_rows, cols], q_t, preferred_element_type=F32)
        row = lax.broadcasted_iota(jnp.int32, s.shape, 0)
        col = lax.broadcasted_iota(jnp.int32, s.shape, 1)
        s = jnp.where(row <= col, s, NEG_BIG)
        m0 = jnp.max(s, axis=0, keepdims=True)
        p = jnp.exp2((s - m0) * c)
        vt_own = jnp.concatenate([v_transposed(own_rows, cols), ones], axis=0).astype(BF16)
        acc_ref[hh] = jnp.dot(vt_own, p.astype(BF16), preferred_element_type=F32)
        m_init.append(m0)

    last_group = nb // grp - 1

    def scores_into(slot, t):
        tt = jnp.minimum(t, last_group)
        keys = pl.ds(pl.multiple_of(tt * (grp * bs), grp * bs), grp * bs)
        for hh in range(heads):
            s_ref[slot, hh] = jnp.dot(k_ref[keys, hh * dh:(hh + 1) * dh], q_ts[hh], preferred_element_type=F32)

    def softmax_pv(slot, t, ms):
        probs, stats = [], []
        for hh in range(heads):
            parts = []
            for u in range(grp):
                picked = sel_ref[hh, pl.ds(t * grp + u, 1), :] > 0.0
                parts.append(jnp.where(picked, s_ref[slot, hh, u * bs:(u + 1) * bs, :], NEG_BIG))
            mx = parts[0]
            for u in range(1, grp):
                mx = jnp.maximum(mx, parts[u])
            m_new = jnp.maximum(ms[hh], jnp.max(mx, axis=0, keepdims=True))
            stats.append((m_new, jnp.exp2((ms[hh] - m_new) * c)))
            probs.append(jnp.concatenate([jnp.exp2((part - m_new) * c).astype(BF16) for part in parts], axis=0))
        tt = jnp.minimum(t, last_group)
        for hh in range(heads):
            pv = jnp.dot(vt_ref[hh, tt], probs[hh], preferred_element_type=F32)
            acc_ref[hh] = stats[hh][1] * acc_ref[hh] + pv
        return tuple(m_new for m_new, _ in stats)

    def body(d, ms):
        t = 2 * d
        scores_into(1, t + 1)
        ms = softmax_pv(0, t, ms)
        scores_into(0, t + 2)
        return softmax_pv(1, t + 1, ms)

    trips = (qi + grp - 1) // grp
    scores_into(0, 0)
    lax.fori_loop(0, (trips + 1) // 2, body, tuple(m_init))
    for hh in range(heads):
        acc = acc_ref[hh]
        o_ref[:, hh * dh:(hh + 1) * dh] = (acc[0:dh] / acc[dh:dh + 1]).T.astype(o_ref.dtype)


def _moba_attention(qkv):
    s = qkv.shape[0]
    bs = MOBA_BLOCK
    nb = s // bs
    heads = MOBA_HEADS_PER_STEP
    assert s % bs == 0 and nb % MOBA_GROUP == 0 and ATTN_HEADS % heads == 0
    nbp = -(-nb // SUBLANES) * SUBLANES
    width = heads * ATTN_HEAD_DIM
    steps = ATTN_HEADS // heads
    return pl.pallas_call(
        functools.partial(_moba_kernel, nb=nb),
        out_shape=jax.ShapeDtypeStruct((s, ATTN_WIDTH), F32),
        grid=(steps, nb),
        in_specs=[pl.BlockSpec((bs, width), lambda h, i: (i, h)),
                  pl.BlockSpec((s, width), lambda h, i: (0, steps + h)),
                  pl.BlockSpec((s, width), lambda h, i: (0, 2 * steps + h))],
        out_specs=pl.BlockSpec((bs, width), lambda h, i: (i, h)),
        scratch_shapes=[pltpu.VMEM((heads, nbp, ATTN_HEAD_DIM), F32),
                        pltpu.VMEM((heads, nb // MOBA_GROUP, ATTN_HEAD_DIM + ONES_ROWS, MOBA_GROUP * bs), BF16),
                        pltpu.VMEM((heads, nbp + SUBLANES, bs), F32),
                        pltpu.VMEM((2, heads, MOBA_GROUP * bs, bs), F32),
                        pltpu.VMEM((heads, ATTN_HEAD_DIM + ONES_ROWS, bs), F32)],
        compiler_params=_params("parallel", "arbitrary"),
        name="moba_attention",
    )(qkv, qkv, qkv)


def _silu(x):
    return x * jax.nn.sigmoid(x)


def _ssd_kernel(z_ref, xs_ref, bc_ref, hb_ref, wdt_ref, cwx_ref, cwbc_ref, cbx_ref, cbbc_ref, dtb_ref, alog_ref,
                dskip_ref, ng_ref, expand_ref, o_ref, xpad_ref, bcpad_ref, state_ref, y_ref):
    L = SSD_CHUNK
    P = SSD_HEAD_DIM
    N = SSD_STATE
    halo = SUBLANES
    c = pl.program_id(0)
    hi = lax.Precision.HIGHEST

    @pl.when(c == 0)
    def _():
        xpad_ref[0:halo, :] = jnp.zeros((halo, xpad_ref.shape[1]), F32)
        bcpad_ref[0:halo, :] = jnp.zeros((halo, bcpad_ref.shape[1]), F32)
        state_ref[...] = jnp.zeros_like(state_ref)

    xpad_ref[halo:halo + L, :] = xs_ref[...]
    bcpad_ref[halo:halo + L, :] = bc_ref[...]

    def conv(pad_ref, w_ref, b_ref):
        acc = b_ref[...]
        for k in range(SSD_CONV):
            off = halo - (SSD_CONV - 1) + k
            acc = acc + pad_ref[off:off + L, :] * w_ref[k:k + 1, :]
        return _silu(acc)

    xs = conv(xpad_ref, cwx_ref, cbx_ref)
    bc = conv(bcpad_ref, cwbc_ref, cbbc_ref)
    xpad_ref[0:halo, :] = xpad_ref[L:L + halo, :]
    bcpad_ref[0:halo, :] = bcpad_ref[L:L + halo, :]

    lane = lax.broadcasted_iota(jnp.int32, (1, LANES), 1)
    dtr = jnp.dot(hb_ref[...], wdt_ref[...], preferred_element_type=F32) + dtb_ref[...]
    dt = jnp.maximum(dtr, 0.0) + jnp.log1p(jnp.exp(-jnp.abs(dtr)))
    a = jnp.where(lane < SSD_HEADS, -jnp.exp(alog_ref[...]), 0.0)
    da = dt * a
    row = lax.broadcasted_iota(jnp.int32, (L, L), 0)
    col = lax.broadcasted_iota(jnp.int32, (L, L), 1)
    tril = row >= col
    a_cs = jnp.dot(tril.astype(F32), da, preferred_element_type=F32, precision=hi)
    a_cs_t = a_cs.T
    exp_a = jnp.exp(a_cs)
    decay = jnp.exp(a_cs[L - 1:L, :] - a_cs)
    stacked = jnp.concatenate([dt, exp_a, decay], axis=0)
    expand = expand_ref[...]
    wide = jnp.zeros((3 * L, SSD_WIDTH), F32)
    rest = stacked
    for _ in range(3):
        piece = rest.astype(BF16)
        wide = wide + jnp.dot(piece, expand, preferred_element_type=F32)
        rest = rest - piece.astype(F32)
    dt_w, exp_w, dec_w = wide[0:L], wide[L:2 * L], wide[2 * L:3 * L]
    xdt = xs * dt_w
    xdec_b = (xdt * dec_w).astype(BF16)
    lane_w = lax.broadcasted_iota(jnp.int32, (L, LANES), 1)
    first_head = lane_w < P

    gw = SSD_WIDTH // SSD_GROUPS
    pairs_per_group = gw // LANES
    for g in range(SSD_GROUPS):
        b_g = bc[:, g * N:(g + 1) * N]
        c_g = bc[:, SSD_GROUPS * N + g * N:SSD_GROUPS * N + (g + 1) * N].astype(BF16)
        b_t = b_g.T.astype(BF16)
        cb = jnp.dot(c_g, b_t, preferred_element_type=F32)
        ssq = jnp.zeros((L, 1), F32)
        for i in range(pairs_per_group):
            pi = g * pairs_per_group + i
            blk = slice(pi * LANES, (pi + 1) * LANES)
            x_blk = xdt[:, blk]
            y_pair = jnp.zeros((L, LANES), F32)
            for e in range(2):
                h = 2 * pi + e
                seg = jnp.exp(jnp.where(tril, a_cs[:, h:h + 1] - a_cs_t[h:h + 1, :], -jnp.inf))
                mat = (cb * seg).astype(BF16)
                keep = first_head if e == 0 else jnp.logical_not(first_head)
                y_pair = y_pair + jnp.dot(mat, jnp.where(keep, x_blk, 0.0).astype(BF16),
                                          preferred_element_type=F32)
            st = jnp.dot(b_t, xdec_b[:, blk], preferred_element_type=F32)
            h_prev = state_ref[pi]
            y_off = jnp.dot(c_g, h_prev.astype(BF16), preferred_element_type=F32) * exp_w[:, blk]
            state_ref[pi] = h_prev * exp_w[L - 1:L, blk] + st
            y = y_pair + y_off + xs[:, blk] * dskip_ref[:, blk]
            yg = y * _silu(z_ref[:, blk])
            ssq = ssq + jnp.sum(yg * yg, axis=-1, keepdims=True)
            y_ref[:, blk] = yg
        rs = lax.rsqrt(ssq / gw + RMS_EPS)
        gsl = slice(g * gw, (g + 1) * gw)
        o_ref[:, gsl] = (y_ref[:, gsl] * rs * ng_ref[:, gsl]).astype(o_ref.dtype)


def _pad_lanes(v, width=LANES):
    return jnp.pad(v.astype(F32), (0, width - v.shape[0])).reshape(1, width)


def _ssd(zx, hb, w_dt, conv_w, conv_b, dt_bias, a_log, d_skip, norm_g):
    s = zx.shape[0]
    d = hb.shape[1]
    w_dt = jnp.pad(w_dt, ((0, 0), (0, DT_PAD - w_dt.shape[1]))).astype(BF16)
    L = SSD_CHUNK
    assert s % L == 0
    w = SSD_WIDTH
    expand = (jnp.arange(LANES)[:, None] == (jnp.arange(w)[None, :] // SSD_HEAD_DIM)).astype(BF16)
    d_wide = jnp.repeat(d_skip.astype(F32), SSD_HEAD_DIM).reshape(1, w)
    row1 = lambda width: pl.BlockSpec((1, width), lambda c: (0, 0))
    return pl.pallas_call(
        _ssd_kernel,
        out_shape=jax.ShapeDtypeStruct((s, w), BF16),
        grid=(s // L,),
        in_specs=[pl.BlockSpec((L, w), lambda c: (c, 0)),
                  pl.BlockSpec((L, w), lambda c: (c, 1)),
                  pl.BlockSpec((L, SSD_BC), lambda c: (c, 2 * w // SSD_BC)),
                  pl.BlockSpec((L, d), lambda c: (c, 0)),
                  pl.BlockSpec((d, DT_PAD), lambda c: (0, 0)),
                  pl.BlockSpec((SSD_CONV, w), lambda c: (0, 0)),
                  pl.BlockSpec((SSD_CONV, SSD_BC), lambda c: (0, 0)),
                  row1(w), row1(SSD_BC), row1(LANES), row1(LANES), row1(w), row1(w),
                  pl.BlockSpec((LANES, w), lambda c: (0, 0))],
        out_specs=pl.BlockSpec((L, w), lambda c: (c, 0)),
        scratch_shapes=[pltpu.VMEM((L + SUBLANES, w), F32),
                        pltpu.VMEM((L + SUBLANES, SSD_BC), F32),
                        pltpu.VMEM((w // LANES, SSD_STATE, LANES), F32),
                        pltpu.VMEM((L, w), F32)],
        compiler_params=_params("arbitrary"),
        name="ssd_scan",
    )(zx, zx, zx, hb, w_dt, conv_w[:, :w], conv_w[:, w:], conv_b[:w].reshape(1, w), conv_b[w:].reshape(1, SSD_BC),
      _pad_lanes(dt_bias), _pad_lanes(a_log), d_wide, norm_g.reshape(1, w), expand)


def _layer_norm(y, g, b):
    mu = jnp.mean(y, axis=-1, keepdims=True)
    d = y - mu
    var = jnp.mean(d * d, axis=-1, keepdims=True)
    return d * lax.rsqrt(var + LN_EPS) * g + b


def _outproj_kernel(attn_ref, ssd_ref, h_ref, w_ref, ag_ref, g_ref, b_ref, o_ref, ob_ref):
    attn = attn_ref[...]
    attn_n = attn * lax.rsqrt(jnp.mean(attn * attn, axis=-1, keepdims=True) + RMS_EPS) * ag_ref[...]
    aw = ATTN_WIDTH
    mix = jnp.dot(attn_n.astype(BF16), w_ref[0:aw, :], preferred_element_type=F32)
    mix = mix + jnp.dot(ssd_ref[...], w_ref[aw:, :], preferred_element_type=F32)
    out = _layer_norm(DEEPNORM_ALPHA * h_ref[...] + mix, g_ref[...], b_ref[...])
    o_ref[...] = out
    ob_ref[...] = out.astype(BF16)


def _outproj_ln(attn, ssd, h, w, layer, attn_g, ln_g, ln_b):
    s, d = h.shape
    tm = _tile(s, 512)
    row = lambda width: pl.BlockSpec((1, width), lambda i: (0, 0))
    return pl.pallas_call(
        _outproj_kernel,
        out_shape=(jax.ShapeDtypeStruct((s, d), F32), jax.ShapeDtypeStruct((s, d), BF16)),
        grid=(s // tm,),
        in_specs=[pl.BlockSpec((tm, ATTN_WIDTH), lambda i: (i, 0)),
                  pl.BlockSpec((tm, SSD_WIDTH), lambda i: (i, 0)),
                  pl.BlockSpec((tm, d), lambda i: (i, 0)),
                  pl.BlockSpec((None, d, d), lambda i: (layer, 0, 0), pipeline_mode=pl.Buffered(1)),
                  row(ATTN_WIDTH), row(d), row(d)],
        out_specs=(pl.BlockSpec((tm, d), lambda i: (i, 0)), pl.BlockSpec((tm, d), lambda i: (i, 0))),
        compiler_params=_params("parallel"),
        name="outproj_ln",
    )(attn, ssd, h, w, attn_g.reshape(1, -1), ln_g.reshape(1, -1), ln_b.reshape(1, -1))


def _ffn_up_kernel(x_ref, wg_ref, wu_ref, cwg_ref, cwu_ref, cbg_ref, cbu_ref, o_ref, gpad_ref, upad_ref,
                   wgb_ref, wub_ref):
    tm = x_ref.shape[0]
    halo = SUBLANES
    _cast_weights_once(wg_ref, wgb_ref)
    _cast_weights_once(wu_ref, wub_ref)

    @pl.when(pl.program_id(1) == 0)
    def _():
        gpad_ref[0:halo, :] = jnp.zeros((halo, gpad_ref.shape[1]), F32)
        upad_ref[0:halo, :] = jnp.zeros((halo, upad_ref.shape[1]), F32)

    x = x_ref[...]
    gpad_ref[halo:halo + tm, :] = jnp.dot(x, wgb_ref[...], preferred_element_type=F32)
    upad_ref[halo:halo + tm, :] = jnp.dot(x, wub_ref[...], preferred_element_type=F32)

    def conv(pad_ref, w_ref, b_ref):
        acc = b_ref[...]
        for k in range(FFN_CONV):
            off = halo - (FFN_CONV - 1) + k
            acc = acc + pad_ref[off:off + tm, :] * w_ref[k:k + 1, :]
        return acc

    g = conv(gpad_ref, cwg_ref, cbg_ref)
    u = conv(upad_ref, cwu_ref, cbu_ref)
    o_ref[...] = (_silu(g) * u).astype(o_ref.dtype)
    gpad_ref[0:halo, :] = gpad_ref[tm:tm + halo, :]
    upad_ref[0:halo, :] = upad_ref[tm:tm + halo, :]


def _ffn_up(hb, w_up, layer, conv_w, conv_b):
    s, d = hb.shape
    tm, tn = _tile(s, 1024), 512
    nj = D_FF // tn
    conv_b = conv_b.reshape(1, -1)
    return pl.pallas_call(
        _ffn_up_kernel,
        out_shape=jax.ShapeDtypeStruct((s, D_FF), BF16),
        grid=(nj, s // tm),
        in_specs=[pl.BlockSpec((tm, d), lambda j, i: (i, 0)),
                  pl.BlockSpec((None, d, tn), lambda j, i: (layer, 0, j)),
                  pl.BlockSpec((None, d, tn), lambda j, i: (layer, 0, j + nj)),
                  pl.BlockSpec((FFN_CONV, tn), lambda j, i: (0, j)),
                  pl.BlockSpec((FFN_CONV, tn), lambda j, i: (0, j + nj)),
                  pl.BlockSpec((1, tn), lambda j, i: (0, j)),
                  pl.BlockSpec((1, tn), lambda j, i: (0, j + nj))],
        out_specs=pl.BlockSpec((tm, tn), lambda j, i: (i, j)),
        scratch_shapes=[pltpu.VMEM((tm + SUBLANES, tn), F32), pltpu.VMEM((tm + SUBLANES, tn), F32),
                        pltpu.VMEM((d, tn), BF16), pltpu.VMEM((d, tn), BF16)],
        compiler_params=_params("parallel", "arbitrary"),
        name="ffn_up",
    )(hb, w_up, w_up, conv_w, conv_w, conv_b, conv_b)


def _ple_kernel(p_ref, hb_ref, wp_ref, wg_ref, o_ref, wpb_ref, wgb_ref):
    _cast_weights_once(wp_ref, wpb_ref)
    _cast_weights_once(wg_ref, wgb_ref)
    proj = jnp.dot(p_ref[...].astype(BF16), wpb_ref[...], preferred_element_type=F32)
    gate = jnp.dot(hb_ref[...], wgb_ref[...], preferred_element_type=F32)
    o_ref[...] = proj * jax.nn.sigmoid(gate)


def _ple(p, hb, w_proj, w_gate, layer):
    s, d = hb.shape
    tm, tn = _tile(s, 1024), 1024
    return pl.pallas_call(
        _ple_kernel,
        out_shape=jax.ShapeDtypeStruct((s, d), F32),
        grid=(d // tn, s // tm),
        in_specs=[pl.BlockSpec((None, None, tm, PE_DIM), lambda j, i: (layer, 0, i, 0)),
                  pl.BlockSpec((tm, d), lambda j, i: (i, 0)),
                  pl.BlockSpec((None, PE_DIM, tn), lambda j, i: (layer, 0, j)),
                  pl.BlockSpec((None, d, tn), lambda j, i: (layer, 0, j))],
        out_specs=pl.BlockSpec((tm, tn), lambda j, i: (i, j)),
        scratch_shapes=[pltpu.VMEM((PE_DIM, tn), BF16), pltpu.VMEM((d, tn), BF16)],
        compiler_params=_params("parallel", "arbitrary"),
        name="ple_gate",
    )(p, hb, w_proj, w_gate)


def _down_kernel(a_ref, w_ref, h_ref, ple_ref, g_ref, b_ref, o_ref, ob_ref):
    ffn = jnp.dot(a_ref[...], w_ref[...], preferred_element_type=F32)
    out = _layer_norm(DEEPNORM_ALPHA * h_ref[...] + ffn + ple_ref[...], g_ref[...], b_ref[...])
    o_ref[...] = out
    ob_ref[...] = out.astype(BF16)


def _down_ln(act, w_down, layer, h, ple, ln_g, ln_b):
    s, d = h.shape
    tm = _tile(s, 256)
    row = lambda width: pl.BlockSpec((1, width), lambda i: (0, 0))
    return pl.pallas_call(
        _down_kernel,
        out_shape=(jax.ShapeDtypeStruct((s, d), F32), jax.ShapeDtypeStruct((s, d), BF16)),
        grid=(s // tm,),
        in_specs=[pl.BlockSpec((tm, D_FF), lambda i: (i, 0)),
                  pl.BlockSpec((None, D_FF, d), lambda i: (layer, 0, 0), pipeline_mode=pl.Buffered(1)),
                  pl.BlockSpec((tm, d), lambda i: (i, 0)),
                  pl.BlockSpec((tm, d), lambda i: (i, 0)),
                  row(d), row(d)],
        out_specs=(pl.BlockSpec((tm, d), lambda i: (i, 0)), pl.BlockSpec((tm, d), lambda i: (i, 0))),
        compiler_params=_params("parallel"),
        name="down_ln",
    )(act, w_down, h, ple, ln_g.reshape(1, -1), ln_b.reshape(1, -1))


def kernel(x, p, positions, w_in, attn_norm_g, ssd_conv_w, ssd_conv_b, ssd_dt_bias, ssd_a_log, ssd_d, ssd_norm_g,
           w_out, ln1_g, ln1_b, w_up, ffn_conv_w, ffn_conv_b, w_down, w_pe_gate, w_pe_proj, ln2_g, ln2_b):
    b, s, d = x.shape
    assert b == 1 and d == D_MODEL
    cos, sin = _rope_tables(positions[0])
    h = x[0]
    hb = h.astype(BF16)
    dt_col = 3 * ATTN_WIDTH + ZX_WIDTH
    w_in_t = jnp.swapaxes(w_in, 1, 2)
    w_out_b = w_out.astype(BF16)
    w_down_b = w_down.astype(BF16)
    for i in range(w_in.shape[0]):
        qkv = _qkv_proj(hb, w_in_t, i, cos, sin)
        zx = _zx_proj(hb, w_in_t, i)
        attn = _moba_attention(qkv)
        ssd = _ssd(zx, hb, w_in[i, :, dt_col:], ssd_conv_w[i], ssd_conv_b[i], ssd_dt_bias[i], ssd_a_log[i],
                   ssd_d[i], ssd_norm_g[i])
        h1, h1b = _outproj_ln(attn, ssd, h, w_out_b, i, attn_norm_g[i], ln1_g[i], ln1_b[i])
        act = _ffn_up(h1b, w_up, i, ffn_conv_w[i], ffn_conv_b[i])
        ple = _ple(p, h1b, w_pe_proj, w_pe_gate, i)
        h, hb = _down_ln(act, w_down_b, i, h1, ple, ln2_g[i], ln2_b[i])
    return h[None]
```

```python
import functools

import jax
import jax.numpy as jnp
from jax import lax
from jax.experimental import pallas as pl
from jax.experimental.pallas import tpu as pltpu

F32 = jnp.float32
BF16 = jnp.bfloat16

D_MODEL = 2048
DEPTH = 4
PE_DIM = 256
ATTN_HEADS = 8
ATTN_HEAD_DIM = 128
ATTN_WIDTH = ATTN_HEADS * ATTN_HEAD_DIM
MOBA_BLOCK = 256
MOBA_TOPK = 3
ROPE_DIM = ATTN_HEAD_DIM // 4
ROPE_THETA = 500000.0
SSD_WIDTH = D_MODEL - ATTN_WIDTH
SSD_HEAD_DIM = 64
SSD_HEADS = SSD_WIDTH // SSD_HEAD_DIM
SSD_GROUPS = 2
SSD_STATE = 128
SSD_CONV = 4
SSD_CHUNK = 256
SSD_BC = 2 * SSD_GROUPS * SSD_STATE
SSD_CONV_DIM = SSD_WIDTH + SSD_BC
D_FF = 5632
FFN_CONV = 3
DEEPNORM_ALPHA = (2 * DEPTH) ** 0.25
LN_EPS = 1e-5
RMS_EPS = 1e-6

LANES = 128
SUBLANES = 8
VMEM_LIMIT = 56 * 1024 * 1024

NEG_BIG = -1e30
DT_PAD = LANES
ZX_WIDTH = SSD_WIDTH + SSD_CONV_DIM


def _params(*sem):
    return pltpu.CompilerParams(dimension_semantics=sem, vmem_limit_bytes=VMEM_LIMIT)


def _tile(n, pref):
    t = min(n, pref)
    assert n % t == 0, (n, t)
    return t


def _rope_table_kernel(pos_ref, invf_ref, sgn_ref, cos_ref, sin_ref):
    ang = pos_ref[...].astype(F32) * invf_ref[...]
    cos_ref[...] = jnp.cos(ang)
    sin_ref[...] = jnp.sin(ang) * sgn_ref[...]


def _rope_tables(positions):
    s = positions.shape[0]
    half = ROPE_DIM // 2
    inv_freq = 1.0 / (ROPE_THETA ** (jnp.arange(0, ROPE_DIM, 2, dtype=F32) / ROPE_DIM))
    zeros = jnp.zeros((LANES - ROPE_DIM,), F32)
    invf = jnp.concatenate([inv_freq, inv_freq, zeros]).reshape(1, LANES)
    sgn = jnp.concatenate([-jnp.ones((half,), F32), jnp.ones((half,), F32), zeros]).reshape(1, LANES)
    tm = _tile(s, 1024)
    return pl.pallas_call(
        _rope_table_kernel,
        out_shape=(jax.ShapeDtypeStruct((s, LANES), F32), jax.ShapeDtypeStruct((s, LANES), F32)),
        grid=(s // tm,),
        in_specs=[pl.BlockSpec((tm, 1), lambda i: (i, 0)),
                  pl.BlockSpec((1, LANES), lambda i: (0, 0)),
                  pl.BlockSpec((1, LANES), lambda i: (0, 0))],
        out_specs=(pl.BlockSpec((tm, LANES), lambda i: (i, 0)),
                   pl.BlockSpec((tm, LANES), lambda i: (i, 0))),
        compiler_params=_params("parallel"),
        name="rope_tables",
    )(positions.reshape(s, 1), invf, sgn)


PROJ_TN = 512


def _cast_weights_once(w_ref, wb_ref):
    @pl.when(pl.program_id(1) == 0)
    def _():
        wb_ref[...] = w_ref[...].astype(BF16)


def _dot_wt(x, wt):
    return lax.dot_general(x, wt, (((1,), (1,)), ((), ())), preferred_element_type=F32)


def _qkv_kernel(x_ref, wt_ref, cos_ref, sin_ref, o_ref, wb_ref, *, rope_tiles):
    _cast_weights_once(wt_ref, wb_ref)
    acc = _dot_wt(x_ref[...], wb_ref[...])
    j = pl.program_id(0)

    @pl.when(j < rope_tiles)
    def _():
        cos = cos_ref[...]
        sin = sin_ref[...]
        lane = lax.broadcasted_iota(jnp.int32, cos.shape, 1)
        half = ROPE_DIM // 2
        for c in range(acc.shape[1] // LANES):
            t = acc[:, c * LANES:(c + 1) * LANES]
            partner = jnp.where(lane < half, pltpu.roll(t, LANES - half, 1), pltpu.roll(t, half, 1))
            o_ref[:, c * LANES:(c + 1) * LANES] = (t * cos + partner * sin).astype(o_ref.dtype)

    @pl.when(j >= rope_tiles)
    def _():
        o_ref[...] = acc.astype(o_ref.dtype)


def _qkv_proj(xb, w_in_t, layer, cos, sin):
    s, k = xb.shape
    n = 3 * ATTN_WIDTH
    tm, tn = _tile(s, 2048), PROJ_TN
    return pl.pallas_call(
        functools.partial(_qkv_kernel, rope_tiles=2 * ATTN_WIDTH // tn),
        out_shape=jax.ShapeDtypeStruct((s, n), BF16),
        grid=(n // tn, s // tm),
        in_specs=[pl.BlockSpec((tm, k), lambda j, i: (i, 0)),
                  pl.BlockSpec((None, tn, k), lambda j, i: (layer, j, 0)),
                  pl.BlockSpec((tm, LANES), lambda j, i: (i, 0)),
                  pl.BlockSpec((tm, LANES), lambda j, i: (i, 0))],
        out_specs=pl.BlockSpec((tm, tn), lambda j, i: (i, j)),
        scratch_shapes=[pltpu.VMEM((tn, k), BF16)],
        compiler_params=_params("parallel", "arbitrary"),
        name="qkv_proj",
    )(xb, w_in_t, cos, sin)


def _zx_kernel(x_ref, wt_ref, o_ref, wb_ref):
    _cast_weights_once(wt_ref, wb_ref)
    o_ref[...] = _dot_wt(x_ref[...], wb_ref[...])


def _zx_proj(xb, w_in_t, layer):
    s, k = xb.shape
    n = ZX_WIDTH
    tm, tn = _tile(s, 2048), PROJ_TN
    first = 3 * ATTN_WIDTH // tn
    return pl.pallas_call(
        _zx_kernel,
        out_shape=jax.ShapeDtypeStruct((s, n), F32),
        grid=(n // tn, s // tm),
        in_specs=[pl.BlockSpec((tm, k), lambda j, i: (i, 0)),
                  pl.BlockSpec((None, tn, k), lambda j, i: (layer, first + j, 0))],
        out_specs=pl.BlockSpec((tm, tn), lambda j, i: (i, j)),
        scratch_shapes=[pltpu.VMEM((tn, k), BF16)],
        compiler_params=_params("parallel", "arbitrary"),
        name="zx_proj",
    )(xb, w_in_t)


MOBA_GROUP = 2
MOBA_HEADS_PER_STEP = 2
ONES_ROWS = 16
LOG2_E = 1.4426950408889634


def _moba_kernel(q_ref, k_ref, v_ref, o_ref, kmean_ref, vt_ref, sel_ref, s_ref, acc_ref, *, nb):
    bs = MOBA_BLOCK
    dh = ATTN_HEAD_DIM
    grp = MOBA_GROUP
    heads = MOBA_HEADS_PER_STEP
    qi = pl.program_id(1)
    c = dh ** -0.5 * LOG2_E
    ones = jnp.ones((ONES_ROWS, bs), F32)

    def v_transposed(rows, cols):
        return v_ref[rows, cols].astype(F32).T

    @pl.when(qi == 0)
    def _():
        kmean_ref[...] = jnp.zeros_like(kmean_ref)
        vt_ref[:, :, dh:, :] = jnp.ones((heads, nb // grp, ONES_ROWS, grp * bs), BF16)

        def body(t, carry):
            for hh in range(heads):
                cols = slice(hh * dh, (hh + 1) * dh)
                for u in range(grp):
                    n = t * grp + u
                    rows = pl.ds(pl.multiple_of(n * bs, bs), bs)
                    kmean_ref[hh, pl.ds(n, 1), :] = jnp.mean(k_ref[rows, cols].astype(F32), axis=0, keepdims=True)
                    vt_ref[hh, t, 0:dh, u * bs:(u + 1) * bs] = v_transposed(rows, cols).astype(BF16)
            return carry

        lax.fori_loop(0, nb // grp, body, 0)

    own_rows = pl.ds(pl.multiple_of(qi * bs, bs), bs)
    head_cols = [slice(hh * dh, (hh + 1) * dh) for hh in range(heads)]
    q_ts = [q_ref[:, cols].astype(F32).T.astype(BF16) for cols in head_cols]

    last_group = nb // grp - 1

    def scores_into(slot, t):
        tt = jnp.minimum(t, last_group)
        keys = pl.ds(pl.multiple_of(tt * (grp * bs), grp * bs), grp * bs)
        for hh in range(heads):
            s_ref[slot, hh] = jnp.dot(k_ref[keys, head_cols[hh]], q_ts[hh], preferred_element_type=F32)

    gates = [jnp.dot(kmean_ref[hh].astype(BF16), q_ts[hh], preferred_element_type=F32)
             for hh in range(heads)]
    own_scores = [jnp.dot(k_ref[own_rows, head_cols[hh]], q_ts[hh], preferred_element_type=F32)
                  for hh in range(heads)]
    scores_into(0, 0)

    m_init = []
    for hh in range(heads):
        blk = lax.broadcasted_iota(jnp.int32, gates[hh].shape, 0)
        g = jnp.where(blk < qi, gates[hh], -jnp.inf)
        sel = jnp.zeros(g.shape, F32)
        for _ in range(MOBA_TOPK):
            mx = jnp.max(g, axis=0, keepdims=True)
            first = jnp.min(jnp.where(g == mx, blk, nb), axis=0, keepdims=True)
            pick = blk == jnp.where(mx > -jnp.inf, first, -1)
            sel = jnp.where(pick, 1.0, sel)
            g = jnp.where(pick, -jnp.inf, g)
        sel_ref[hh] = jnp.concatenate([sel, jnp.zeros((SUBLANES, bs), F32)], axis=0)

        s = own_scores[hh]
        row = lax.broadcasted_iota(jnp.int32, s.shape, 0)
        col = lax.broadcasted_iota(jnp.int32, s.shape, 1)
        s = jnp.where(row <= col, s, NEG_BIG)
        m0 = jnp.max(s, axis=0, keepdims=True)
        p = jnp.exp2((s - m0) * c)
        vt_own = jnp.concatenate([v_transposed(own_rows, head_cols[hh]), ones], axis=0).astype(BF16)
        acc_ref[hh] = jnp.dot(vt_own, p.astype(BF16), preferred_element_type=F32)
        m_init.append(m0)

    def softmax_pv(slot, t, ms):
        probs, stats = [], []
        for hh in range(heads):
            parts = []
            for u in range(grp):
                picked = sel_ref[hh, pl.ds(t * grp + u, 1), :] > 0.0
                parts.append(jnp.where(picked, s_ref[slot, hh, u * bs:(u + 1) * bs, :], NEG_BIG))
            mx = parts[0]
            for u in range(1, grp):
                mx = jnp.maximum(mx, parts[u])
            m_new = jnp.maximum(ms[hh], jnp.max(mx, axis=0, keepdims=True))
            stats.append((m_new, jnp.exp2((ms[hh] - m_new) * c)))
            probs.append(jnp.concatenate([jnp.exp2((part - m_new) * c).astype(BF16) for part in parts], axis=0))
        tt = jnp.minimum(t, last_group)
        for hh in range(heads):
            pv = jnp.dot(vt_ref[hh, tt], probs[hh], preferred_element_type=F32)
            acc_ref[hh] = stats[hh][1] * acc_ref[hh] + pv
        return tuple(m_new for m_new, _ in stats)

    def body(d, ms):
        t = 2 * d
        scores_into(1, t + 1)
        ms = softmax_pv(0, t, ms)
        scores_into(0, t + 2)
        return softmax_pv(1, t + 1, ms)

    trips = (qi + grp - 1) // grp
    lax.fori_loop(0, (trips + 1) // 2, body, tuple(m_init))
    for hh in range(heads):
        acc = acc_ref[hh]
        o_ref[:, hh * dh:(hh + 1) * dh] = (acc[0:dh] / acc[dh:dh + 1]).T.astype(o_ref.dtype)


def _moba_attention(qkv):
    s = qkv.shape[0]
    bs = MOBA_BLOCK
    nb = s // bs
    heads = MOBA_HEADS_PER_STEP
    assert s % bs == 0 and nb % MOBA_GROUP == 0 and ATTN_HEADS % heads == 0
    nbp = -(-nb // SUBLANES) * SUBLANES
    width = heads * ATTN_HEAD_DIM
    steps = ATTN_HEADS // heads
    return pl.pallas_call(
        functools.partial(_moba_kernel, nb=nb),
        out_shape=jax.ShapeDtypeStruct((s, ATTN_WIDTH), F32),
        grid=(steps, nb),
        in_specs=[pl.BlockSpec((bs, width), lambda h, i: (i, h)),
                  pl.BlockSpec((s, width), lambda h, i: (0, steps + h)),
                  pl.BlockSpec((s, width), lambda h, i: (0, 2 * steps + h))],
        out_specs=pl.BlockSpec((bs, width), lambda h, i: (i, h)),
        scratch_shapes=[pltpu.VMEM((heads, nbp, ATTN_HEAD_DIM), F32),
                        pltpu.VMEM((heads, nb // MOBA_GROUP, ATTN_HEAD_DIM + ONES_ROWS, MOBA_GROUP * bs), BF16),
                        pltpu.VMEM((heads, nbp + SUBLANES, bs), F32),
                        pltpu.VMEM((2, heads, MOBA_GROUP * bs, bs), F32),
                        pltpu.VMEM((heads, ATTN_HEAD_DIM + ONES_ROWS, bs), F32)],
        compiler_params=_params("parallel", "arbitrary"),
        name="moba_attention",
    )(qkv, qkv, qkv)


def _silu(x):
    return x * jax.nn.sigmoid(x)


def _ssd_kernel(z_ref, xs_ref, bc_ref, hb_ref, wdt_ref, cwx_ref, cwbc_ref, cbx_ref, cbbc_ref, dtb_ref, alog_ref,
                dskip_ref, ng_ref, expand_ref, o_ref, xpad_ref, bcpad_ref, state_ref, y_ref):
    L = SSD_CHUNK
    P = SSD_HEAD_DIM
    N = SSD_STATE
    halo = SUBLANES
    c = pl.program_id(0)
    hi = lax.Precision.HIGHEST

    @pl.when(c == 0)
    def _():
        xpad_ref[0:halo, :] = jnp.zeros((halo, xpad_ref.shape[1]), F32)
        bcpad_ref[0:halo, :] = jnp.zeros((halo, bcpad_ref.shape[1]), F32)
        state_ref[...] = jnp.zeros_like(state_ref)

    xpad_ref[halo:halo + L, :] = xs_ref[...]
    bcpad_ref[halo:halo + L, :] = bc_ref[...]

    def conv(pad_ref, w_ref, b_ref):
        acc = b_ref[...]
        for k in range(SSD_CONV):
            off = halo - (SSD_CONV - 1) + k
            acc = acc + pad_ref[off:off + L, :] * w_ref[k:k + 1, :]
        return _silu(acc)

    xs = conv(xpad_ref, cwx_ref, cbx_ref)
    bc = conv(bcpad_ref, cwbc_ref, cbbc_ref)
    xpad_ref[0:halo, :] = xpad_ref[L:L + halo, :]
    bcpad_ref[0:halo, :] = bcpad_ref[L:L + halo, :]

    lane = lax.broadcasted_iota(jnp.int32, (1, LANES), 1)
    dtr = jnp.dot(hb_ref[...], wdt_ref[...], preferred_element_type=F32) + dtb_ref[...]
    dt = jnp.maximum(dtr, 0.0) + jnp.log1p(jnp.exp(-jnp.abs(dtr)))
    a = jnp.where(lane < SSD_HEADS, -jnp.exp(alog_ref[...]), 0.0)
    da = dt * a
    row = lax.broadcasted_iota(jnp.int32, (L, L), 0)
    col = lax.broadcasted_iota(jnp.int32, (L, L), 1)
    tril = row >= col
    a_cs = jnp.dot(tril.astype(F32), da, preferred_element_type=F32, precision=hi)
    a_cs_t = a_cs.T
    exp_a = jnp.exp(a_cs)
    decay = jnp.exp(a_cs[L - 1:L, :] - a_cs)
    stacked = jnp.concatenate([dt, exp_a, decay], axis=0)
    expand = expand_ref[...]
    wide = jnp.zeros((3 * L, SSD_WIDTH), F32)
    rest = stacked
    for _ in range(3):
        piece = rest.astype(BF16)
        wide = wide + jnp.dot(piece, expand, preferred_element_type=F32)
        rest = rest - piece.astype(F32)
    dt_w, exp_w, dec_w = wide[0:L], wide[L:2 * L], wide[2 * L:3 * L]
    xdt = xs * dt_w
    xdec_b = (xdt * dec_w).astype(BF16)
    lane_w = lax.broadcasted_iota(jnp.int32, (L, LANES), 1)
    first_head = lane_w < P

    gw = SSD_WIDTH // SSD_GROUPS
    pairs_per_group = gw // LANES
    for g in range(SSD_GROUPS):
        b_g = bc[:, g * N:(g + 1) * N]
        c_g = bc[:, SSD_GROUPS * N + g * N:SSD_GROUPS * N + (g + 1) * N].astype(BF16)
        b_t = b_g.T.astype(BF16)
        cb = jnp.dot(c_g, b_t, preferred_element_type=F32)
        ssq = jnp.zeros((L, 1), F32)
        for i in range(pairs_per_group):
            pi = g * pairs_per_group + i
            blk = slice(pi * LANES, (pi + 1) * LANES)
            x_blk = xdt[:, blk]
            y_pair = jnp.zeros((L, LANES), F32)
            for e in range(2):
                h = 2 * pi + e
                seg = jnp.exp(jnp.where(tril, a_cs[:, h:h + 1] - a_cs_t[h:h + 1, :], -jnp.inf))
                mat = (cb * seg).astype(BF16)
                keep = first_head if e == 0 else jnp.logical_not(first_head)
                y_pair = y_pair + jnp.dot(mat, jnp.where(keep, x_blk, 0.0).astype(BF16),
                                          preferred_element_type=F32)
            st = jnp.dot(b_t, xdec_b[:, blk], preferred_element_type=F32)
            h_prev = state_ref[pi]
            y_off = jnp.dot(c_g, h_prev.astype(BF16), preferred_element_type=F32) * exp_w[:, blk]
            state_ref[pi] = h_prev * exp_w[L - 1:L, blk] + st
            y = y_pair + y_off + xs[:, blk] * dskip_ref[:, blk]
            yg = y * _silu(z_ref[:, blk])
            ssq = ssq + jnp.sum(yg * yg, axis=-1, keepdims=True)
            y_ref[:, blk] = yg
        rs = lax.rsqrt(ssq / gw + RMS_EPS)
        gsl = slice(g * gw, (g + 1) * gw)
        o_ref[:, gsl] = (y_ref[:, gsl] * rs * ng_ref[:, gsl]).astype(o_ref.dtype)


def _pad_lanes(v, width=LANES):
    return jnp.pad(v.astype(F32), (0, width - v.shape[0])).reshape(1, width)


def _ssd(zx, hb, w_dt, conv_w, conv_b, dt_bias, a_log, d_skip, norm_g):
    s = zx.shape[0]
    d = hb.shape[1]
    w_dt = jnp.pad(w_dt, ((0, 0), (0, DT_PAD - w_dt.shape[1]))).astype(BF16)
    L = SSD_CHUNK
    assert s % L == 0
    w = SSD_WIDTH
    expand = (jnp.arange(LANES)[:, None] == (jnp.arange(w)[None, :] // SSD_HEAD_DIM)).astype(BF16)
    d_wide = jnp.repeat(d_skip.astype(F32), SSD_HEAD_DIM).reshape(1, w)
    row1 = lambda width: pl.BlockSpec((1, width), lambda c: (0, 0))
    return pl.pallas_call(
        _ssd_kernel,
        out_shape=jax.ShapeDtypeStruct((s, w), BF16),
        grid=(s // L,),
        in_specs=[pl.BlockSpec((L, w), lambda c: (c, 0)),
                  pl.BlockSpec((L, w), lambda c: (c, 1)),
                  pl.BlockSpec((L, SSD_BC), lambda c: (c, 2 * w // SSD_BC)),
                  pl.BlockSpec((L, d), lambda c: (c, 0)),
                  pl.BlockSpec((d, DT_PAD), lambda c: (0, 0)),
                  pl.BlockSpec((SSD_CONV, w), lambda c: (0, 0)),
                  pl.BlockSpec((SSD_CONV, SSD_BC), lambda c: (0, 0)),
                  row1(w), row1(SSD_BC), row1(LANES), row1(LANES), row1(w), row1(w),
                  pl.BlockSpec((LANES, w), lambda c: (0, 0))],
        out_specs=pl.BlockSpec((L, w), lambda c: (c, 0)),
        scratch_shapes=[pltpu.VMEM((L + SUBLANES, w), F32),
                        pltpu.VMEM((L + SUBLANES, SSD_BC), F32),
                        pltpu.VMEM((w // LANES, SSD_STATE, LANES), F32),
                        pltpu.VMEM((L, w), F32)],
        compiler_params=_params("arbitrary"),
        name="ssd_scan",
    )(zx, zx, zx, hb, w_dt, conv_w[:, :w], conv_w[:, w:], conv_b[:w].reshape(1, w), conv_b[w:].reshape(1, SSD_BC),
      _pad_lanes(dt_bias), _pad_lanes(a_log), d_wide, norm_g.reshape(1, w), expand)


def _layer_norm(y, g, b):
    mu = jnp.mean(y, axis=-1, keepdims=True)
    d = y - mu
    var = jnp.mean(d * d, axis=-1, keepdims=True)
    return d * lax.rsqrt(var + LN_EPS) * g + b


def _outproj_kernel(attn_ref, ssd_ref, h_ref, w_ref, ag_ref, g_ref, b_ref, o_ref, ob_ref):
    attn = attn_ref[...]
    attn_n = attn * lax.rsqrt(jnp.mean(attn * attn, axis=-1, keepdims=True) + RMS_EPS) * ag_ref[...]
    aw = ATTN_WIDTH
    mix = jnp.dot(attn_n.astype(BF16), w_ref[0:aw, :], preferred_element_type=F32)
    mix = mix + jnp.dot(ssd_ref[...], w_ref[aw:, :], preferred_element_type=F32)
    out = _layer_norm(DEEPNORM_ALPHA * h_ref[...] + mix, g_ref[...], b_ref[...])
    o_ref[...] = out
    ob_ref[...] = out.astype(BF16)


def _outproj_ln(attn, ssd, h, w, layer, attn_g, ln_g, ln_b):
    s, d = h.shape
    tm = _tile(s, 512)
    row = lambda width: pl.BlockSpec((1, width), lambda i: (0, 0))
    return pl.pallas_call(
        _outproj_kernel,
        out_shape=(jax.ShapeDtypeStruct((s, d), F32), jax.ShapeDtypeStruct((s, d), BF16)),
        grid=(s // tm,),
        in_specs=[pl.BlockSpec((tm, ATTN_WIDTH), lambda i: (i, 0)),
                  pl.BlockSpec((tm, SSD_WIDTH), lambda i: (i, 0)),
                  pl.BlockSpec((tm, d), lambda i: (i, 0)),
                  pl.BlockSpec((None, d, d), lambda i: (layer, 0, 0), pipeline_mode=pl.Buffered(1)),
                  row(ATTN_WIDTH), row(d), row(d)],
        out_specs=(pl.BlockSpec((tm, d), lambda i: (i, 0)), pl.BlockSpec((tm, d), lambda i: (i, 0))),
        compiler_params=_params("parallel"),
        name="outproj_ln",
    )(attn, ssd, h, w, attn_g.reshape(1, -1), ln_g.reshape(1, -1), ln_b.reshape(1, -1))


def _ffn_up_kernel(x_ref, wg_ref, wu_ref, cwg_ref, cwu_ref, cbg_ref, cbu_ref, o_ref, gpad_ref, upad_ref,
                   wgb_ref, wub_ref):
    tm = x_ref.shape[0]
    halo = SUBLANES
    _cast_weights_once(wg_ref, wgb_ref)
    _cast_weights_once(wu_ref, wub_ref)

    @pl.when(pl.program_id(1) == 0)
    def _():
        gpad_ref[0:halo, :] = jnp.zeros((halo, gpad_ref.shape[1]), F32)
        upad_ref[0:halo, :] = jnp.zeros((halo, upad_ref.shape[1]), F32)

    x = x_ref[...]
    gpad_ref[halo:halo + tm, :] = jnp.dot(x, wgb_ref[...], preferred_element_type=F32)
    upad_ref[halo:halo + tm, :] = jnp.dot(x, wub_ref[...], preferred_element_type=F32)

    def conv(pad_ref, w_ref, b_ref):
        acc = b_ref[...]
        for k in range(FFN_CONV):
            off = halo - (FFN_CONV - 1) + k
            acc = acc + pad_ref[off:off + tm, :] * w_ref[k:k + 1, :]
        return acc

    g = conv(gpad_ref, cwg_ref, cbg_ref)
    u = conv(upad_ref, cwu_ref, cbu_ref)
    o_ref[...] = (_silu(g) * u).astype(o_ref.dtype)
    gpad_ref[0:halo, :] = gpad_ref[tm:tm + halo, :]
    upad_ref[0:halo, :] = upad_ref[tm:tm + halo, :]


def _ffn_up(hb, w_up, layer, conv_w, conv_b):
    s, d = hb.shape
    tm, tn = _tile(s, 1024), 512
    nj = D_FF // tn
    conv_b = conv_b.reshape(1, -1)
    return pl.pallas_call(
        _ffn_up_kernel,
        out_shape=jax.ShapeDtypeStruct((s, D_FF), BF16),
        grid=(nj, s // tm),
        in_specs=[pl.BlockSpec((tm, d), lambda j, i: (i, 0)),
                  pl.BlockSpec((None, d, tn), lambda j, i: (layer, 0, j)),
                  pl.BlockSpec((None, d, tn), lambda j, i: (layer, 0, j + nj)),
                  pl.BlockSpec((FFN_CONV, tn), lambda j, i: (0, j)),
                  pl.BlockSpec((FFN_CONV, tn), lambda j, i: (0, j + nj)),
                  pl.BlockSpec((1, tn), lambda j, i: (0, j)),
                  pl.BlockSpec((1, tn), lambda j, i: (0, j + nj))],
        out_specs=pl.BlockSpec((tm, tn), lambda j, i: (i, j)),
        scratch_shapes=[pltpu.VMEM((tm + SUBLANES, tn), F32), pltpu.VMEM((tm + SUBLANES, tn), F32),
                        pltpu.VMEM((d, tn), BF16), pltpu.VMEM((d, tn), BF16)],
        compiler_params=_params("parallel", "arbitrary"),
        name="ffn_up",
    )(hb, w_up, w_up, conv_w, conv_w, conv_b, conv_b)


def _ple_kernel(p_ref, hb_ref, wp_ref, wg_ref, o_ref, wpb_ref, wgb_ref):
    _cast_weights_once(wp_ref, wpb_ref)
    _cast_weights_once(wg_ref, wgb_ref)
    proj = jnp.dot(p_ref[...].astype(BF16), wpb_ref[...], preferred_element_type=F32)
    gate = jnp.dot(hb_ref[...], wgb_ref[...], preferred_element_type=F32)
    o_ref[...] = proj * jax.nn.sigmoid(gate)


def _ple(p, hb, w_proj, w_gate, layer):
    s, d = hb.shape
    tm, tn = _tile(s, 1024), 1024
    return pl.pallas_call(
        _ple_kernel,
        out_shape=jax.ShapeDtypeStruct((s, d), F32),
        grid=(d // tn, s // tm),
        in_specs=[pl.BlockSpec((None, None, tm, PE_DIM), lambda j, i: (layer, 0, i, 0)),
                  pl.BlockSpec((tm, d), lambda j, i: (i, 0)),
                  pl.BlockSpec((None, PE_DIM, tn), lambda j, i: (layer, 0, j)),
                  pl.BlockSpec((None, d, tn), lambda j, i: (layer, 0, j))],
        out_specs=pl.BlockSpec((tm, tn), lambda j, i: (i, j)),
        scratch_shapes=[pltpu.VMEM((PE_DIM, tn), BF16), pltpu.VMEM((d, tn), BF16)],
        compiler_params=_params("parallel", "arbitrary"),
        name="ple_gate",
    )(p, hb, w_proj, w_gate)


def _down_kernel(a_ref, w_ref, h_ref, ple_ref, g_ref, b_ref, o_ref, ob_ref):
    ffn = jnp.dot(a_ref[...], w_ref[...], preferred_element_type=F32)
    out = _layer_norm(DEEPNORM_ALPHA * h_ref[...] + ffn + ple_ref[...], g_ref[...], b_ref[...])
    o_ref[...] = out
    ob_ref[...] = out.astype(BF16)


def _down_ln(act, w_down, layer, h, ple, ln_g, ln_b):
    s, d = h.shape
    tm = _tile(s, 256)
    row = lambda width: pl.BlockSpec((1, width), lambda i: (0, 0))
    return pl.pallas_call(
        _down_kernel,
        out_shape=(jax.ShapeDtypeStruct((s, d), F32), jax.ShapeDtypeStruct((s, d), BF16)),
        grid=(s // tm,),
        in_specs=[pl.BlockSpec((tm, D_FF), lambda i: (i, 0)),
                  pl.BlockSpec((None, D_FF, d), lambda i: (layer, 0, 0), pipeline_mode=pl.Buffered(1)),
                  pl.BlockSpec((tm, d), lambda i: (i, 0)),
                  pl.BlockSpec((tm, d), lambda i: (i, 0)),
                  row(d), row(d)],
        out_specs=(pl.BlockSpec((tm, d), lambda i: (i, 0)), pl.BlockSpec((tm, d), lambda i: (i, 0))),
        compiler_params=_params("parallel"),
        name="down_ln",
    )(act, w_down, h, ple, ln_g.reshape(1, -1), ln_b.reshape(1, -1))


def kernel(x, p, positions, w_in, attn_norm_g, ssd_conv_w, ssd_conv_b, ssd_dt_bias, ssd_a_log, ssd_d, ssd_norm_g,
           w_out, ln1_g, ln1_b, w_up, ffn_conv_w, ffn_conv_b, w_down, w_pe_gate, w_pe_proj, ln2_g, ln2_b):
    b, s, d = x.shape
    assert b == 1 and d == D_MODEL
    cos, sin = _rope_tables(positions[0])
    h = x[0]
    hb = h.astype(BF16)
    dt_col = 3 * ATTN_WIDTH + ZX_WIDTH
    w_in_t = jnp.swapaxes(w_in, 1, 2)
    w_out_b = w_out.astype(BF16)
    w_down_b = w_down.astype(BF16)
    for i in range(w_in.shape[0]):
        qkv = _qkv_proj(hb, w_in_t, i, cos, sin)
        zx = _zx_proj(hb, w_in_t, i)
        attn = _moba_attention(qkv)
        ssd = _ssd(zx, hb, w_in_t[i, dt_col:, :].T, ssd_conv_w[i], ssd_conv_b[i], ssd_dt_bias[i], ssd_a_log[i],
                   ssd_d[i], ssd_norm_g[i])
        h1, h1b = _outproj_ln(attn, ssd, h, w_out_b, i, attn_norm_g[i], ln1_g[i], ln1_b[i])
        act = _ffn_up(h1b, w_up, i, ffn_conv_w[i], ffn_conv_b[i])
        ple = _ple(p, h1b, w_pe_proj, w_pe_gate, i)
        h, hb = _down_ln(act, w_down_b, i, h1, ple, ln2_g[i], ln2_b[i])
    return h[None]
```

```python
import functools

import jax
import jax.numpy as jnp
from jax import lax
from jax.experimental import pallas as pl
from jax.experimental.pallas import tpu as pltpu

F32 = jnp.float32
BF16 = jnp.bfloat16

D_MODEL = 2048
DEPTH = 4
PE_DIM = 256
ATTN_HEADS = 8
ATTN_HEAD_DIM = 128
ATTN_WIDTH = ATTN_HEADS * ATTN_HEAD_DIM
MOBA_BLOCK = 256
MOBA_TOPK = 3
ROPE_DIM = ATTN_HEAD_DIM // 4
ROPE_THETA = 500000.0
SSD_WIDTH = D_MODEL - ATTN_WIDTH
SSD_HEAD_DIM = 64
SSD_HEADS = SSD_WIDTH // SSD_HEAD_DIM
SSD_GROUPS = 2
SSD_STATE = 128
SSD_CONV = 4
SSD_CHUNK = 256
SSD_BC = 2 * SSD_GROUPS * SSD_STATE
SSD_CONV_DIM = SSD_WIDTH + SSD_BC
D_FF = 5632
FFN_CONV = 3
DEEPNORM_ALPHA = (2 * DEPTH) ** 0.25
LN_EPS = 1e-5
RMS_EPS = 1e-6

LANES = 128
SUBLANES = 8
VMEM_LIMIT = 56 * 1024 * 1024

NEG_BIG = -1e30
DT_PAD = LANES
ZX_WIDTH = SSD_WIDTH + SSD_CONV_DIM


def _params(*sem):
    return pltpu.CompilerParams(dimension_semantics=sem, vmem_limit_bytes=VMEM_LIMIT)


def _tile(n, pref):
    t = min(n, pref)
    assert n % t == 0, (n, t)
    return t


def _rope_table_kernel(pos_ref, invf_ref, sgn_ref, cos_ref, sin_ref):
    ang = pos_ref[...].astype(F32) * invf_ref[...]
    cos_ref[...] = jnp.cos(ang)
    sin_ref[...] = jnp.sin(ang) * sgn_ref[...]


def _rope_tables(positions):
    s = positions.shape[0]
    half = ROPE_DIM // 2
    inv_freq = 1.0 / (ROPE_THETA ** (jnp.arange(0, ROPE_DIM, 2, dtype=F32) / ROPE_DIM))
    zeros = jnp.zeros((LANES - ROPE_DIM,), F32)
    invf = jnp.concatenate([inv_freq, inv_freq, zeros]).reshape(1, LANES)
    sgn = jnp.concatenate([-jnp.ones((half,), F32), jnp.ones((half,), F32), zeros]).reshape(1, LANES)
    tm = _tile(s, 1024)
    return pl.pallas_call(
        _rope_table_kernel,
        out_shape=(jax.ShapeDtypeStruct((s, LANES), F32), jax.ShapeDtypeStruct((s, LANES), F32)),
        grid=(s // tm,),
        in_specs=[pl.BlockSpec((tm, 1), lambda i: (i, 0)),
                  pl.BlockSpec((1, LANES), lambda i: (0, 0)),
                  pl.BlockSpec((1, LANES), lambda i: (0, 0))],
        out_specs=(pl.BlockSpec((tm, LANES), lambda i: (i, 0)),
                   pl.BlockSpec((tm, LANES), lambda i: (i, 0))),
        compiler_params=_params("parallel"),
        name="rope_tables",
    )(positions.reshape(s, 1), invf, sgn)


PROJ_TN = 512


def _cast_weights_once(w_ref, wb_ref):
    @pl.when(pl.program_id(1) == 0)
    def _():
        wb_ref[...] = w_ref[...].astype(BF16)


def _dot_wt(x, wt):
    return lax.dot_general(x, wt, (((1,), (1,)), ((), ())), preferred_element_type=F32)


def _qkv_kernel(x_ref, wt_ref, cos_ref, sin_ref, o_ref, wb_ref, *, rope_tiles):
    _cast_weights_once(wt_ref, wb_ref)
    acc = _dot_wt(x_ref[...], wb_ref[...])
    j = pl.program_id(0)

    @pl.when(j < rope_tiles)
    def _():
        cos = cos_ref[...]
        sin = sin_ref[...]
        lane = lax.broadcasted_iota(jnp.int32, cos.shape, 1)
        half = ROPE_DIM // 2
        for c in range(acc.shape[1] // LANES):
            t = acc[:, c * LANES:(c + 1) * LANES]
            partner = jnp.where(lane < half, pltpu.roll(t, LANES - half, 1), pltpu.roll(t, half, 1))
            o_ref[:, c * LANES:(c + 1) * LANES] = (t * cos + partner * sin).astype(o_ref.dtype)

    @pl.when(j >= rope_tiles)
    def _():
        o_ref[...] = acc.astype(o_ref.dtype)


def _qkv_proj(xb, w_in_t, layer, cos, sin):
    s, k = xb.shape
    n = 3 * ATTN_WIDTH
    tm, tn = _tile(s, 2048), PROJ_TN
    return pl.pallas_call(
        functools.partial(_qkv_kernel, rope_tiles=2 * ATTN_WIDTH // tn),
        out_shape=jax.ShapeDtypeStruct((s, n), BF16),
        grid=(n // tn, s // tm),
        in_specs=[pl.BlockSpec((tm, k), lambda j, i: (i, 0)),
                  pl.BlockSpec((None, tn, k), lambda j, i: (layer, j, 0)),
                  pl.BlockSpec((tm, LANES), lambda j, i: (i, 0)),
                  pl.BlockSpec((tm, LANES), lambda j, i: (i, 0))],
        out_specs=pl.BlockSpec((tm, tn), lambda j, i: (i, j)),
        scratch_shapes=[pltpu.VMEM((tn, k), BF16)],
        compiler_params=_params("parallel", "arbitrary"),
        name="qkv_proj",
    )(xb, w_in_t, cos, sin)


def _zx_kernel(x_ref, wt_ref, o_ref, wb_ref):
    _cast_weights_once(wt_ref, wb_ref)
    o_ref[...] = _dot_wt(x_ref[...], wb_ref[...])


def _zx_proj(xb, w_in_t, layer):
    s, k = xb.shape
    n = ZX_WIDTH
    tm, tn = _tile(s, 2048), PROJ_TN
    first = 3 * ATTN_WIDTH // tn
    return pl.pallas_call(
        _zx_kernel,
        out_shape=jax.ShapeDtypeStruct((s, n), F32),
        grid=(n // tn, s // tm),
        in_specs=[pl.BlockSpec((tm, k), lambda j, i: (i, 0)),
                  pl.BlockSpec((None, tn, k), lambda j, i: (layer, first + j, 0))],
        out_specs=pl.BlockSpec((tm, tn), lambda j, i: (i, j)),
        scratch_shapes=[pltpu.VMEM((tn, k), BF16)],
        compiler_params=_params("parallel", "arbitrary"),
        name="zx_proj",
    )(xb, w_in_t)


MOBA_GROUP = 2
MOBA_HEADS_PER_STEP = 4
ONES_ROWS = 16
LOG2_E = 1.4426950408889634


def _moba_kernel(q_ref, k_ref, v_ref, o_ref, kmean_ref, vt_ref, sel_ref, s_ref, acc_ref, *, nb):
    bs = MOBA_BLOCK
    dh = ATTN_HEAD_DIM
    grp = MOBA_GROUP
    heads = MOBA_HEADS_PER_STEP
    qi = pl.program_id(1)
    c = dh ** -0.5 * LOG2_E
    ones = jnp.ones((ONES_ROWS, bs), F32)

    def v_transposed(rows, cols):
        return v_ref[rows, cols].astype(F32).T

    @pl.when(qi == 0)
    def _():
        kmean_ref[...] = jnp.zeros_like(kmean_ref)
        vt_ref[:, :, dh:, :] = jnp.ones((heads, nb // grp, ONES_ROWS, grp * bs), BF16)

        def body(t, carry):
            for hh in range(heads):
                cols = slice(hh * dh, (hh + 1) * dh)
                for u in range(grp):
                    n = t * grp + u
                    rows = pl.ds(pl.multiple_of(n * bs, bs), bs)
                    kmean_ref[hh, pl.ds(n, 1), :] = jnp.mean(k_ref[rows, cols].astype(F32), axis=0, keepdims=True)
                    vt_ref[hh, t, 0:dh, u * bs:(u + 1) * bs] = v_transposed(rows, cols).astype(BF16)
            return carry

        lax.fori_loop(0, nb // grp, body, 0)

    own_rows = pl.ds(pl.multiple_of(qi * bs, bs), bs)
    head_cols = [slice(hh * dh, (hh + 1) * dh) for hh in range(heads)]
    q_ts = [q_ref[:, cols].astype(F32).T.astype(BF16) for cols in head_cols]

    last_group = nb // grp - 1

    def scores_into(slot, t):
        tt = jnp.minimum(t, last_group)
        keys = pl.ds(pl.multiple_of(tt * (grp * bs), grp * bs), grp * bs)
        for hh in range(heads):
            s_ref[slot, hh] = jnp.dot(k_ref[keys, head_cols[hh]], q_ts[hh], preferred_element_type=F32)

    gates = [jnp.dot(kmean_ref[hh].astype(BF16), q_ts[hh], preferred_element_type=F32)
             for hh in range(heads)]
    own_scores = [jnp.dot(k_ref[own_rows, head_cols[hh]], q_ts[hh], preferred_element_type=F32)
                  for hh in range(heads)]
    scores_into(0, 0)

    m_init = []
    for hh in range(heads):
        blk = lax.broadcasted_iota(jnp.int32, gates[hh].shape, 0)
        g = jnp.where(blk < qi, gates[hh], -jnp.inf)
        sel = jnp.zeros(g.shape, F32)
        for _ in range(MOBA_TOPK):
            mx = jnp.max(g, axis=0, keepdims=True)
            first = jnp.min(jnp.where(g == mx, blk, nb), axis=0, keepdims=True)
            pick = blk == jnp.where(mx > -jnp.inf, first, -1)
            sel = jnp.where(pick, 1.0, sel)
            g = jnp.where(pick, -jnp.inf, g)
        sel_ref[hh] = jnp.concatenate([sel, jnp.zeros((SUBLANES, bs), F32)], axis=0)

        s = own_scores[hh]
        row = lax.broadcasted_iota(jnp.int32, s.shape, 0)
        col = lax.broadcasted_iota(jnp.int32, s.shape, 1)
        s = jnp.where(row <= col, s, NEG_BIG)
        m0 = jnp.max(s, axis=0, keepdims=True)
        p = jnp.exp2((s - m0) * c)
        vt_own = jnp.concatenate([v_transposed(own_rows, head_cols[hh]), ones], axis=0).astype(BF16)
        acc_ref[hh] = jnp.dot(vt_own, p.astype(BF16), preferred_element_type=F32)
        m_init.append(m0)

    def softmax_pv(slot, t, ms):
        probs, stats = [], []
        for hh in range(heads):
            parts = []
            for u in range(grp):
                picked = sel_ref[hh, pl.ds(t * grp + u, 1), :] > 0.0
                parts.append(jnp.where(picked, s_ref[slot, hh, u * bs:(u + 1) * bs, :], NEG_BIG))
            mx = parts[0]
            for u in range(1, grp):
                mx = jnp.maximum(mx, parts[u])
            m_new = jnp.maximum(ms[hh], jnp.max(mx, axis=0, keepdims=True))
            stats.append((m_new, jnp.exp2((ms[hh] - m_new) * c)))
            probs.append(jnp.concatenate([jnp.exp2((part - m_new) * c).astype(BF16) for part in parts], axis=0))
        tt = jnp.minimum(t, last_group)
        for hh in range(heads):
            pv = jnp.dot(vt_ref[hh, tt], probs[hh], preferred_element_type=F32)
            acc_ref[hh] = stats[hh][1] * acc_ref[hh] + pv
        return tuple(m_new for m_new, _ in stats)

    def body(d, ms):
        t = 2 * d
        scores_into(1, t + 1)
        ms = softmax_pv(0, t, ms)
        scores_into(0, t + 2)
        return softmax_pv(1, t + 1, ms)

    trips = (qi + grp - 1) // grp
    lax.fori_loop(0, (trips + 1) // 2, body, tuple(m_init))
    for hh in range(heads):
        acc = acc_ref[hh]
        o_ref[:, hh * dh:(hh + 1) * dh] = (acc[0:dh] / acc[dh:dh + 1]).T.astype(o_ref.dtype)


def _moba_attention(qkv):
    s = qkv.shape[0]
    bs = MOBA_BLOCK
    nb = s // bs
    heads = MOBA_HEADS_PER_STEP
    assert s % bs == 0 and nb % MOBA_GROUP == 0 and ATTN_HEADS % heads == 0
    nbp = -(-nb // SUBLANES) * SUBLANES
    width = heads * ATTN_HEAD_DIM
    steps = ATTN_HEADS // heads
    return pl.pallas_call(
        functools.partial(_moba_kernel, nb=nb),
        out_shape=jax.ShapeDtypeStruct((s, ATTN_WIDTH), F32),
        grid=(steps, nb),
        in_specs=[pl.BlockSpec((bs, width), lambda h, i: (i, h)),
                  pl.BlockSpec((s, width), lambda h, i: (0, steps + h)),
                  pl.BlockSpec((s, width), lambda h, i: (0, 2 * steps + h))],
        out_specs=pl.BlockSpec((bs, width), lambda h, i: (i, h)),
        scratch_shapes=[pltpu.VMEM((heads, nbp, ATTN_HEAD_DIM), F32),
                        pltpu.VMEM((heads, nb // MOBA_GROUP, ATTN_HEAD_DIM + ONES_ROWS, MOBA_GROUP * bs), BF16),
                        pltpu.VMEM((heads, nbp + SUBLANES, bs), F32),
                        pltpu.VMEM((2, heads, MOBA_GROUP * bs, bs), F32),
                        pltpu.VMEM((heads, ATTN_HEAD_DIM + ONES_ROWS, bs), F32)],
        compiler_params=_params("parallel", "arbitrary"),
        name="moba_attention",
    )(qkv, qkv, qkv)


def _silu(x):
    return x * jax.nn.sigmoid(x)


def _ssd_kernel(z_ref, xs_ref, bc_ref, hb_ref, wdt_ref, cwx_ref, cwbc_ref, cbx_ref, cbbc_ref, dtb_ref, alog_ref,
                dskip_ref, ng_ref, expand_ref, o_ref, xpad_ref, bcpad_ref, state_ref, y_ref):
    L = SSD_CHUNK
    P = SSD_HEAD_DIM
    N = SSD_STATE
    halo = SUBLANES
    c = pl.program_id(0)
    hi = lax.Precision.HIGHEST

    @pl.when(c == 0)
    def _():
        xpad_ref[0:halo, :] = jnp.zeros((halo, xpad_ref.shape[1]), F32)
        bcpad_ref[0:halo, :] = jnp.zeros((halo, bcpad_ref.shape[1]), F32)
        state_ref[...] = jnp.zeros_like(state_ref)

    xpad_ref[halo:halo + L, :] = xs_ref[...]
    bcpad_ref[halo:halo + L, :] = bc_ref[...]

    def conv(pad_ref, w_ref, b_ref):
        acc = b_ref[...]
        for k in range(SSD_CONV):
            off = halo - (SSD_CONV - 1) + k
            acc = acc + pad_ref[off:off + L, :] * w_ref[k:k + 1, :]
        return _silu(acc)

    xs = conv(xpad_ref, cwx_ref, cbx_ref)
    bc = conv(bcpad_ref, cwbc_ref, cbbc_ref)
    xpad_ref[0:halo, :] = xpad_ref[L:L + halo, :]
    bcpad_ref[0:halo, :] = bcpad_ref[L:L + halo, :]

    lane = lax.broadcasted_iota(jnp.int32, (1, LANES), 1)
    dtr = jnp.dot(hb_ref[...], wdt_ref[...].astype(BF16), preferred_element_type=F32) + dtb_ref[...]
    dt = jnp.maximum(dtr, 0.0) + jnp.log1p(jnp.exp(-jnp.abs(dtr)))
    a = jnp.where(lane < SSD_HEADS, -jnp.exp(alog_ref[...]), 0.0)
    da = dt * a
    row = lax.broadcasted_iota(jnp.int32, (L, L), 0)
    col = lax.broadcasted_iota(jnp.int32, (L, L), 1)
    tril = row >= col
    a_cs = jnp.dot(tril.astype(F32), da, preferred_element_type=F32, precision=hi)
    a_cs_t = a_cs.T
    exp_a = jnp.exp(a_cs)
    decay = jnp.exp(a_cs[L - 1:L, :] - a_cs)
    stacked = jnp.concatenate([dt, exp_a, decay], axis=0)
    expand = expand_ref[...]
    wide = jnp.zeros((3 * L, SSD_WIDTH), F32)
    rest = stacked
    for _ in range(3):
        piece = rest.astype(BF16)
        wide = wide + jnp.dot(piece, expand, preferred_element_type=F32)
        rest = rest - piece.astype(F32)
    dt_w, exp_w, dec_w = wide[0:L], wide[L:2 * L], wide[2 * L:3 * L]
    xdt = xs * dt_w
    xdec_b = (xdt * dec_w).astype(BF16)
    lane_w = lax.broadcasted_iota(jnp.int32, (L, LANES), 1)
    first_head = lane_w < P

    gw = SSD_WIDTH // SSD_GROUPS
    pairs_per_group = gw // LANES
    for g in range(SSD_GROUPS):
        b_g = bc[:, g * N:(g + 1) * N]
        c_g = bc[:, SSD_GROUPS * N + g * N:SSD_GROUPS * N + (g + 1) * N].astype(BF16)
        b_t = b_g.T.astype(BF16)
        cb = jnp.dot(c_g, b_t, preferred_element_type=F32)
        ssq = jnp.zeros((L, 1), F32)
        for i in range(pairs_per_group):
            pi = g * pairs_per_group + i
            blk = slice(pi * LANES, (pi + 1) * LANES)
            x_blk = xdt[:, blk]
            y_pair = jnp.zeros((L, LANES), F32)
            for e in range(2):
                h = 2 * pi + e
                seg = jnp.exp(jnp.where(tril, a_cs[:, h:h + 1] - a_cs_t[h:h + 1, :], -jnp.inf))
                mat = (cb * seg).astype(BF16)
                keep = first_head if e == 0 else jnp.logical_not(first_head)
                y_pair = y_pair + jnp.dot(mat, jnp.where(keep, x_blk, 0.0).astype(BF16),
                                          preferred_element_type=F32)
            st = jnp.dot(b_t, xdec_b[:, blk], preferred_element_type=F32)
            h_prev = state_ref[pi]
            y_off = jnp.dot(c_g, h_prev.astype(BF16), preferred_element_type=F32) * exp_w[:, blk]
            state_ref[pi] = h_prev * exp_w[L - 1:L, blk] + st
            y = y_pair + y_off + xs[:, blk] * dskip_ref[:, blk]
            yg = y * _silu(z_ref[:, blk])
            ssq = ssq + jnp.sum(yg * yg, axis=-1, keepdims=True)
            y_ref[:, blk] = yg
        rs = lax.rsqrt(ssq / gw + RMS_EPS)
        gsl = slice(g * gw, (g + 1) * gw)
        o_ref[:, gsl] = (y_ref[:, gsl] * rs * ng_ref[:, gsl]).astype(o_ref.dtype)


def _pad_lanes(v, width=LANES):
    return jnp.pad(v.astype(F32), (0, width - v.shape[0])).reshape(1, width)


def _ssd(zx, hb, w_dt, conv_w, conv_b, dt_bias, a_log, d_skip, norm_g):
    s = zx.shape[0]
    d = hb.shape[1]
    w_dt = jnp.pad(w_dt, ((0, 0), (0, DT_PAD - w_dt.shape[1])))
    L = SSD_CHUNK
    assert s % L == 0
    w = SSD_WIDTH
    expand = (jnp.arange(LANES)[:, None] == (jnp.arange(w)[None, :] // SSD_HEAD_DIM)).astype(BF16)
    d_wide = jnp.repeat(d_skip.astype(F32), SSD_HEAD_DIM).reshape(1, w)
    row1 = lambda width: pl.BlockSpec((1, width), lambda c: (0, 0))
    return pl.pallas_call(
        _ssd_kernel,
        out_shape=jax.ShapeDtypeStruct((s, w), BF16),
        grid=(s // L,),
        in_specs=[pl.BlockSpec((L, w), lambda c: (c, 0)),
                  pl.BlockSpec((L, w), lambda c: (c, 1)),
                  pl.BlockSpec((L, SSD_BC), lambda c: (c, 2 * w // SSD_BC)),
                  pl.BlockSpec((L, d), lambda c: (c, 0)),
                  pl.BlockSpec((d, DT_PAD), lambda c: (0, 0)),
                  pl.BlockSpec((SSD_CONV, w), lambda c: (0, 0)),
                  pl.BlockSpec((SSD_CONV, SSD_BC), lambda c: (0, 0)),
                  row1(w), row1(SSD_BC), row1(LANES), row1(LANES), row1(w), row1(w),
                  pl.BlockSpec((LANES, w), lambda c: (0, 0))],
        out_specs=pl.BlockSpec((L, w), lambda c: (c, 0)),
        scratch_shapes=[pltpu.VMEM((L + SUBLANES, w), F32),
                        pltpu.VMEM((L + SUBLANES, SSD_BC), F32),
                        pltpu.VMEM((w // LANES, SSD_STATE, LANES), F32),
                        pltpu.VMEM((L, w), F32)],
        compiler_params=_params("arbitrary"),
        name="ssd_scan",
    )(zx, zx, zx, hb, w_dt, conv_w[:, :w], conv_w[:, w:], conv_b[:w].reshape(1, w), conv_b[w:].reshape(1, SSD_BC),
      _pad_lanes(dt_bias), _pad_lanes(a_log), d_wide, norm_g.reshape(1, w), expand)


def _layer_norm(y, g, b):
    mu = jnp.mean(y, axis=-1, keepdims=True)
    d = y - mu
    var = jnp.mean(d * d, axis=-1, keepdims=True)
    return d * lax.rsqrt(var + LN_EPS) * g + b


def _outproj_kernel(attn_ref, ssd_ref, h_ref, w_ref, ag_ref, g_ref, b_ref, o_ref, ob_ref):
    attn = attn_ref[...]
    attn_n = attn * lax.rsqrt(jnp.mean(attn * attn, axis=-1, keepdims=True) + RMS_EPS) * ag_ref[...]
    aw = ATTN_WIDTH
    mix = jnp.dot(attn_n.astype(BF16), w_ref[0:aw, :], preferred_element_type=F32)
    mix = mix + jnp.dot(ssd_ref[...], w_ref[aw:, :], preferred_element_type=F32)
    out = _layer_norm(DEEPNORM_ALPHA * h_ref[...] + mix, g_ref[...], b_ref[...])
    o_ref[...] = out
    ob_ref[...] = out.astype(BF16)


def _outproj_ln(attn, ssd, h, w, layer, attn_g, ln_g, ln_b):
    s, d = h.shape
    tm = _tile(s, 512)
    row = lambda width: pl.BlockSpec((1, width), lambda i: (0, 0))
    return pl.pallas_call(
        _outproj_kernel,
        out_shape=(jax.ShapeDtypeStruct((s, d), F32), jax.ShapeDtypeStruct((s, d), BF16)),
        grid=(s // tm,),
        in_specs=[pl.BlockSpec((tm, ATTN_WIDTH), lambda i: (i, 0)),
                  pl.BlockSpec((tm, SSD_WIDTH), lambda i: (i, 0)),
                  pl.BlockSpec((tm, d), lambda i: (i, 0)),
                  pl.BlockSpec((None, d, d), lambda i: (layer, 0, 0), pipeline_mode=pl.Buffered(1)),
                  row(ATTN_WIDTH), row(d), row(d)],
        out_specs=(pl.BlockSpec((tm, d), lambda i: (i, 0)), pl.BlockSpec((tm, d), lambda i: (i, 0))),
        compiler_params=_params("parallel"),
        name="outproj_ln",
    )(attn, ssd, h, w, attn_g.reshape(1, -1), ln_g.reshape(1, -1), ln_b.reshape(1, -1))


def _ffn_up_kernel(x_ref, wg_ref, wu_ref, cwg_ref, cwu_ref, cbg_ref, cbu_ref, o_ref, gpad_ref, upad_ref,
                   wgb_ref, wub_ref):
    tm = x_ref.shape[0]
    halo = SUBLANES
    _cast_weights_once(wg_ref, wgb_ref)
    _cast_weights_once(wu_ref, wub_ref)

    @pl.when(pl.program_id(1) == 0)
    def _():
        gpad_ref[0:halo, :] = jnp.zeros((halo, gpad_ref.shape[1]), F32)
        upad_ref[0:halo, :] = jnp.zeros((halo, upad_ref.shape[1]), F32)

    x = x_ref[...]
    gpad_ref[halo:halo + tm, :] = jnp.dot(x, wgb_ref[...], preferred_element_type=F32)
    upad_ref[halo:halo + tm, :] = jnp.dot(x, wub_ref[...], preferred_element_type=F32)

    def conv(pad_ref, w_ref, b_ref):
        acc = b_ref[...]
        for k in range(FFN_CONV):
            off = halo - (FFN_CONV - 1) + k
            acc = acc + pad_ref[off:off + tm, :] * w_ref[k:k + 1, :]
        return acc

    g = conv(gpad_ref, cwg_ref, cbg_ref)
    u = conv(upad_ref, cwu_ref, cbu_ref)
    o_ref[...] = (_silu(g) * u).astype(o_ref.dtype)
    gpad_ref[0:halo, :] = gpad_ref[tm:tm + halo, :]
    upad_ref[0:halo, :] = upad_ref[tm:tm + halo, :]


def _ffn_up(hb, w_up, layer, conv_w, conv_b):
    s, d = hb.shape
    tm, tn = _tile(s, 1024), 512
    nj = D_FF // tn
    conv_b = conv_b.reshape(1, -1)
    return pl.pallas_call(
        _ffn_up_kernel,
        out_shape=jax.ShapeDtypeStruct((s, D_FF), BF16),
        grid=(nj, s // tm),
        in_specs=[pl.BlockSpec((tm, d), lambda j, i: (i, 0)),
                  pl.BlockSpec((None, d, tn), lambda j, i: (layer, 0, j)),
                  pl.BlockSpec((None, d, tn), lambda j, i: (layer, 0, j + nj)),
                  pl.BlockSpec((FFN_CONV, tn), lambda j, i: (0, j)),
                  pl.BlockSpec((FFN_CONV, tn), lambda j, i: (0, j + nj)),
                  pl.BlockSpec((1, tn), lambda j, i: (0, j)),
                  pl.BlockSpec((1, tn), lambda j, i: (0, j + nj))],
        out_specs=pl.BlockSpec((tm, tn), lambda j, i: (i, j)),
        scratch_shapes=[pltpu.VMEM((tm + SUBLANES, tn), F32), pltpu.VMEM((tm + SUBLANES, tn), F32),
                        pltpu.VMEM((d, tn), BF16), pltpu.VMEM((d, tn), BF16)],
        compiler_params=_params("parallel", "arbitrary"),
        name="ffn_up",
    )(hb, w_up, w_up, conv_w, conv_w, conv_b, conv_b)


def _ple_kernel(p_ref, hb_ref, wp_ref, wg_ref, o_ref, wpb_ref, wgb_ref):
    _cast_weights_once(wp_ref, wpb_ref)
    _cast_weights_once(wg_ref, wgb_ref)
    proj = jnp.dot(p_ref[...].astype(BF16), wpb_ref[...], preferred_element_type=F32)
    gate = jnp.dot(hb_ref[...], wgb_ref[...], preferred_element_type=F32)
    o_ref[...] = proj * jax.nn.sigmoid(gate)


def _ple(p, hb, w_proj, w_gate, layer):
    s, d = hb.shape
    tm, tn = _tile(s, 1024), 1024
    return pl.pallas_call(
        _ple_kernel,
        out_shape=jax.ShapeDtypeStruct((s, d), F32),
        grid=(d // tn, s // tm),
        in_specs=[pl.BlockSpec((None, None, tm, PE_DIM), lambda j, i: (layer, 0, i, 0)),
                  pl.BlockSpec((tm, d), lambda j, i: (i, 0)),
                  pl.BlockSpec((None, PE_DIM, tn), lambda j, i: (layer, 0, j)),
                  pl.BlockSpec((None, d, tn), lambda j, i: (layer, 0, j))],
        out_specs=pl.BlockSpec((tm, tn), lambda j, i: (i, j)),
        scratch_shapes=[pltpu.VMEM((PE_DIM, tn), BF16), pltpu.VMEM((d, tn), BF16)],
        compiler_params=_params("parallel", "arbitrary"),
        name="ple_gate",
    )(p, hb, w_proj, w_gate)


def _down_kernel(a_ref, w_ref, h_ref, ple_ref, g_ref, b_ref, o_ref, ob_ref):
    ffn = jnp.dot(a_ref[...], w_ref[...], preferred_element_type=F32)
    out = _layer_norm(DEEPNORM_ALPHA * h_ref[...] + ffn + ple_ref[...], g_ref[...], b_ref[...])
    o_ref[...] = out
    ob_ref[...] = out.astype(BF16)


def _down_ln(act, w_down, layer, h, ple, ln_g, ln_b):
    s, d = h.shape
    tm = _tile(s, 256)
    row = lambda width: pl.BlockSpec((1, width), lambda i: (0, 0))
    return pl.pallas_call(
        _down_kernel,
        out_shape=(jax.ShapeDtypeStruct((s, d), F32), jax.ShapeDtypeStruct((s, d), BF16)),
        grid=(s // tm,),
        in_specs=[pl.BlockSpec((tm, D_FF), lambda i: (i, 0)),
                  pl.BlockSpec((None, D_FF, d), lambda i: (layer, 0, 0), pipeline_mode=pl.Buffered(1)),
                  pl.BlockSpec((tm, d), lambda i: (i, 0)),
                  pl.BlockSpec((tm, d), lambda i: (i, 0)),
                  row(d), row(d)],
        out_specs=(pl.BlockSpec((tm, d), lambda i: (i, 0)), pl.BlockSpec((tm, d), lambda i: (i, 0))),
        compiler_params=_params("parallel"),
        name="down_ln",
    )(act, w_down, h, ple, ln_g.reshape(1, -1), ln_b.reshape(1, -1))


def kernel(x, p, positions, w_in, attn_norm_g, ssd_conv_w, ssd_conv_b, ssd_dt_bias, ssd_a_log, ssd_d, ssd_norm_g,
           w_out, ln1_g, ln1_b, w_up, ffn_conv_w, ffn_conv_b, w_down, w_pe_gate, w_pe_proj, ln2_g, ln2_b):
    b, s, d = x.shape
    assert b == 1 and d == D_MODEL
    cos, sin = _rope_tables(positions[0])
    h = x[0]
    hb = h.astype(BF16)
    dt_col = 3 * ATTN_WIDTH + ZX_WIDTH
    w_in_t = jnp.swapaxes(w_in, 1, 2)
    w_out_b = w_out.astype(BF16)
    w_down_b = w_down.astype(BF16)
    for i in range(w_in.shape[0]):
        qkv = _qkv_proj(hb, w_in_t, i, cos, sin)
        zx = _zx_proj(hb, w_in_t, i)
        attn = _moba_attention(qkv)
        ssd = _ssd(zx, hb, w_in_t[i, dt_col:, :].T, ssd_conv_w[i], ssd_conv_b[i], ssd_dt_bias[i], ssd_a_log[i],
                   ssd_d[i], ssd_norm_g[i])
        h1, h1b = _outproj_ln(attn, ssd, h, w_out_b, i, attn_norm_g[i], ln1_g[i], ln1_b[i])
        act = _ffn_up(h1b, w_up, i, ffn_conv_w[i], ffn_conv_b[i])
        ple = _ple(p, h1b, w_pe_proj, w_pe_gate, i)
        h, hb = _down_ln(act, w_down_b, i, h1, ple, ln2_g[i], ln2_b[i])
    return h[None]
```

```python
import functools

import jax
import jax.numpy as jnp
from jax import lax
from jax.experimental import pallas as pl
from jax.experimental.pallas import tpu as pltpu

F32 = jnp.float32
BF16 = jnp.bfloat16

D_MODEL = 2048
DEPTH = 4
PE_DIM = 256
ATTN_HEADS = 8
ATTN_HEAD_DIM = 128
ATTN_WIDTH = ATTN_HEADS * ATTN_HEAD_DIM
MOBA_BLOCK = 256
MOBA_TOPK = 3
ROPE_DIM = ATTN_HEAD_DIM // 4
ROPE_THETA = 500000.0
SSD_WIDTH = D_MODEL - ATTN_WIDTH
SSD_HEAD_DIM = 64
SSD_HEADS = SSD_WIDTH // SSD_HEAD_DIM
SSD_GROUPS = 2
SSD_STATE = 128
SSD_CONV = 4
SSD_CHUNK = 256
SSD_BC = 2 * SSD_GROUPS * SSD_STATE
SSD_CONV_DIM = SSD_WIDTH + SSD_BC
D_FF = 5632
FFN_CONV = 3
DEEPNORM_ALPHA = (2 * DEPTH) ** 0.25
LN_EPS = 1e-5
RMS_EPS = 1e-6

LANES = 128
SUBLANES = 8
VMEM_LIMIT = 56 * 1024 * 1024

NEG_BIG = -1e30
ZX_WIDTH = SSD_WIDTH + SSD_CONV_DIM


def _params(*sem):
    return pltpu.CompilerParams(dimension_semantics=sem, vmem_limit_bytes=VMEM_LIMIT)


def _tile(n, pref):
    t = min(n, pref)
    assert n % t == 0, (n, t)
    return t


def _rope_table_kernel(pos_ref, invf_ref, sgn_ref, cos_ref, sin_ref):
    ang = pos_ref[...].astype(F32) * invf_ref[...]
    cos_ref[...] = jnp.cos(ang)
    sin_ref[...] = jnp.sin(ang) * sgn_ref[...]


def _rope_tables(positions):
    s = positions.shape[0]
    half = ROPE_DIM // 2
    inv_freq = 1.0 / (ROPE_THETA ** (jnp.arange(0, ROPE_DIM, 2, dtype=F32) / ROPE_DIM))
    zeros = jnp.zeros((LANES - ROPE_DIM,), F32)
    invf = jnp.concatenate([inv_freq, inv_freq, zeros]).reshape(1, LANES)
    sgn = jnp.concatenate([-jnp.ones((half,), F32), jnp.ones((half,), F32), zeros]).reshape(1, LANES)
    tm = _tile(s, 1024)
    return pl.pallas_call(
        _rope_table_kernel,
        out_shape=(jax.ShapeDtypeStruct((s, LANES), F32), jax.ShapeDtypeStruct((s, LANES), F32)),
        grid=(s // tm,),
        in_specs=[pl.BlockSpec((tm, 1), lambda i: (i, 0)),
                  pl.BlockSpec((1, LANES), lambda i: (0, 0)),
                  pl.BlockSpec((1, LANES), lambda i: (0, 0))],
        out_specs=(pl.BlockSpec((tm, LANES), lambda i: (i, 0)),
                   pl.BlockSpec((tm, LANES), lambda i: (i, 0))),
        compiler_params=_params("parallel"),
        name="rope_tables",
    )(positions.reshape(s, 1), invf, sgn)


PROJ_TN = 512


def _cast_weights_once(w_ref, wb_ref):
    @pl.when(pl.program_id(1) == 0)
    def _():
        wb_ref[...] = w_ref[...].astype(BF16)


def _dot_wt(x, wt):
    return lax.dot_general(x, wt, (((1,), (1,)), ((), ())), preferred_element_type=F32)


def _qkv_kernel(x_ref, wt_ref, cos_ref, sin_ref, o_ref, wb_ref, *, rope_tiles):
    _cast_weights_once(wt_ref, wb_ref)
    acc = _dot_wt(x_ref[...], wb_ref[...])
    j = pl.program_id(0)

    @pl.when(j < rope_tiles)
    def _():
        cos = cos_ref[...]
        sin = sin_ref[...]
        lane = lax.broadcasted_iota(jnp.int32, cos.shape, 1)
        half = ROPE_DIM // 2
        for c in range(acc.shape[1] // LANES):
            t = acc[:, c * LANES:(c + 1) * LANES]
            partner = jnp.where(lane < half, pltpu.roll(t, LANES - half, 1), pltpu.roll(t, half, 1))
            o_ref[:, c * LANES:(c + 1) * LANES] = (t * cos + partner * sin).astype(o_ref.dtype)

    @pl.when(j >= rope_tiles)
    def _():
        o_ref[...] = acc.astype(o_ref.dtype)


def _qkv_proj(xb, w_in_t, layer, cos, sin):
    s, k = xb.shape
    n = 3 * ATTN_WIDTH
    tm, tn = _tile(s, 2048), PROJ_TN
    return pl.pallas_call(
        functools.partial(_qkv_kernel, rope_tiles=2 * ATTN_WIDTH // tn),
        out_shape=jax.ShapeDtypeStruct((s, n), BF16),
        grid=(n // tn, s // tm),
        in_specs=[pl.BlockSpec((tm, k), lambda j, i: (i, 0)),
                  pl.BlockSpec((None, tn, k), lambda j, i: (layer, j, 0)),
                  pl.BlockSpec((tm, LANES), lambda j, i: (i, 0)),
                  pl.BlockSpec((tm, LANES), lambda j, i: (i, 0))],
        out_specs=pl.BlockSpec((tm, tn), lambda j, i: (i, j)),
        scratch_shapes=[pltpu.VMEM((tn, k), BF16)],
        compiler_params=_params("parallel", "arbitrary"),
        name="qkv_proj",
    )(xb, w_in_t, cos, sin)


def _zx_kernel(x_ref, wt_ref, o_ref, wb_ref):
    _cast_weights_once(wt_ref, wb_ref)
    o_ref[...] = _dot_wt(x_ref[...], wb_ref[...])


def _zx_proj(xb, w_in_t, layer):
    s, k = xb.shape
    n = ZX_WIDTH
    tm, tn = _tile(s, 2048), PROJ_TN
    first = 3 * ATTN_WIDTH // tn
    return pl.pallas_call(
        _zx_kernel,
        out_shape=jax.ShapeDtypeStruct((s, n), F32),
        grid=(n // tn, s // tm),
        in_specs=[pl.BlockSpec((tm, k), lambda j, i: (i, 0)),
                  pl.BlockSpec((None, tn, k), lambda j, i: (layer, first + j, 0))],
        out_specs=pl.BlockSpec((tm, tn), lambda j, i: (i, j)),
        scratch_shapes=[pltpu.VMEM((tn, k), BF16)],
        compiler_params=_params("parallel", "arbitrary"),
        name="zx_proj",
    )(xb, w_in_t)


MOBA_GROUP = 2
MOBA_HEADS_PER_STEP = 4
ONES_ROWS = 16
LOG2_E = 1.4426950408889634


def _moba_kernel(q_ref, k_ref, v_ref, o_ref, kmean_ref, vt_ref, sel_ref, s_ref, acc_ref, *, nb):
    bs = MOBA_BLOCK
    dh = ATTN_HEAD_DIM
    grp = MOBA_GROUP
    heads = MOBA_HEADS_PER_STEP
    qi = pl.program_id(1)
    c = dh ** -0.5 * LOG2_E
    ones = jnp.ones((ONES_ROWS, bs), F32)

    def v_transposed(rows, cols):
        return v_ref[rows, cols].astype(F32).T

    @pl.when(qi == 0)
    def _():
        kmean_ref[...] = jnp.zeros_like(kmean_ref)
        vt_ref[:, :, dh:, :] = jnp.ones((heads, nb // grp, ONES_ROWS, grp * bs), BF16)

        def body(t, carry):
            for hh in range(heads):
                cols = slice(hh * dh, (hh + 1) * dh)
                for u in range(grp):
                    n = t * grp + u
                    rows = pl.ds(pl.multiple_of(n * bs, bs), bs)
                    kmean_ref[hh, pl.ds(n, 1), :] = jnp.mean(k_ref[rows, cols].astype(F32), axis=0, keepdims=True)
                    vt_ref[hh, t, 0:dh, u * bs:(u + 1) * bs] = v_transposed(rows, cols).astype(BF16)
            return carry

        lax.fori_loop(0, nb // grp, body, 0)

    own_rows = pl.ds(pl.multiple_of(qi * bs, bs), bs)
    head_cols = [slice(hh * dh, (hh + 1) * dh) for hh in range(heads)]
    q_ts = [q_ref[:, cols].astype(F32).T.astype(BF16) for cols in head_cols]

    last_group = nb // grp - 1

    def scores_into(slot, t):
        tt = jnp.minimum(t, last_group)
        keys = pl.ds(pl.multiple_of(tt * (grp * bs), grp * bs), grp * bs)
        for hh in range(heads):
            s_ref[slot, hh] = jnp.dot(k_ref[keys, head_cols[hh]], q_ts[hh], preferred_element_type=F32)

    gates = [jnp.dot(kmean_ref[hh].astype(BF16), q_ts[hh], preferred_element_type=F32)
             for hh in range(heads)]
    own_scores = [jnp.dot(k_ref[own_rows, head_cols[hh]], q_ts[hh], preferred_element_type=F32)
                  for hh in range(heads)]
    scores_into(0, 0)

    m_init = []
    for hh in range(heads):
        blk = lax.broadcasted_iota(jnp.int32, gates[hh].shape, 0)
        g = jnp.where(blk < qi, gates[hh], -jnp.inf)
        sel = jnp.zeros(g.shape, F32)
        for _ in range(MOBA_TOPK):
            mx = jnp.max(g, axis=0, keepdims=True)
            first = jnp.min(jnp.where(g == mx, blk, nb), axis=0, keepdims=True)
            pick = blk == jnp.where(mx > -jnp.inf, first, -1)
            sel = jnp.where(pick, 1.0, sel)
            g = jnp.where(pick, -jnp.inf, g)
        sel_ref[hh] = jnp.concatenate([sel, jnp.zeros((SUBLANES, bs), F32)], axis=0)

        s = own_scores[hh]
        row = lax.broadcasted_iota(jnp.int32, s.shape, 0)
        col = lax.broadcasted_iota(jnp.int32, s.shape, 1)
        s = jnp.where(row <= col, s, NEG_BIG)
        m0 = jnp.max(s, axis=0, keepdims=True)
        p = jnp.exp2((s - m0) * c)
        vt_own = jnp.concatenate([v_transposed(own_rows, head_cols[hh]), ones], axis=0).astype(BF16)
        acc_ref[hh] = jnp.dot(vt_own, p.astype(BF16), preferred_element_type=F32)
        m_init.append(m0)

    def softmax_pv(slot, t, ms):
        probs, stats = [], []
        for hh in range(heads):
            parts = []
            for u in range(grp):
                picked = sel_ref[hh, pl.ds(t * grp + u, 1), :] > 0.0
                parts.append(jnp.where(picked, s_ref[slot, hh, u * bs:(u + 1) * bs, :], NEG_BIG))
            mx = parts[0]
            for u in range(1, grp):
                mx = jnp.maximum(mx, parts[u])
            m_new = jnp.maximum(ms[hh], jnp.max(mx, axis=0, keepdims=True))
            stats.append((m_new, jnp.exp2((ms[hh] - m_new) * c)))
            probs.append(jnp.concatenate([jnp.exp2((part - m_new) * c).astype(BF16) for part in parts], axis=0))
        tt = jnp.minimum(t, last_group)
        for hh in range(heads):
            pv = jnp.dot(vt_ref[hh, tt], probs[hh], preferred_element_type=F32)
            acc_ref[hh] = stats[hh][1] * acc_ref[hh] + pv
        return tuple(m_new for m_new, _ in stats)

    def body(d, ms):
        t = 2 * d
        scores_into(1, t + 1)
        ms = softmax_pv(0, t, ms)
        scores_into(0, t + 2)
        return softmax_pv(1, t + 1, ms)

    trips = (qi + grp - 1) // grp
    lax.fori_loop(0, (trips + 1) // 2, body, tuple(m_init))
    for hh in range(heads):
        acc = acc_ref[hh]
        o_ref[:, hh * dh:(hh + 1) * dh] = (acc[0:dh] / acc[dh:dh + 1]).T.astype(o_ref.dtype)


def _moba_attention(qkv):
    s = qkv.shape[0]
    bs = MOBA_BLOCK
    nb = s // bs
    heads = MOBA_HEADS_PER_STEP
    assert s % bs == 0 and nb % MOBA_GROUP == 0 and ATTN_HEADS % heads == 0
    nbp = -(-nb // SUBLANES) * SUBLANES
    width = heads * ATTN_HEAD_DIM
    steps = ATTN_HEADS // heads
    return pl.pallas_call(
        functools.partial(_moba_kernel, nb=nb),
        out_shape=jax.ShapeDtypeStruct((s, ATTN_WIDTH), F32),
        grid=(steps, nb),
        in_specs=[pl.BlockSpec((bs, width), lambda h, i: (i, h)),
                  pl.BlockSpec((s, width), lambda h, i: (0, steps + h)),
                  pl.BlockSpec((s, width), lambda h, i: (0, 2 * steps + h))],
        out_specs=pl.BlockSpec((bs, width), lambda h, i: (i, h)),
        scratch_shapes=[pltpu.VMEM((heads, nbp, ATTN_HEAD_DIM), F32),
                        pltpu.VMEM((heads, nb // MOBA_GROUP, ATTN_HEAD_DIM + ONES_ROWS, MOBA_GROUP * bs), BF16),
                        pltpu.VMEM((heads, nbp + SUBLANES, bs), F32),
                        pltpu.VMEM((2, heads, MOBA_GROUP * bs, bs), F32),
                        pltpu.VMEM((heads, ATTN_HEAD_DIM + ONES_ROWS, bs), F32)],
        compiler_params=_params("parallel", "arbitrary"),
        name="moba_attention",
    )(qkv, qkv, qkv)


def _silu(x):
    return x * jax.nn.sigmoid(x)


def _ssd_kernel(z_ref, xs_ref, bc_ref, hb_ref, wdt_ref, cwx_ref, cwbc_ref, cbx_ref, cbbc_ref, dtb_ref, alog_ref,
                dskip_ref, ng_ref, expand_ref, o_ref, xpad_ref, bcpad_ref, state_ref, y_ref):
    L = SSD_CHUNK
    P = SSD_HEAD_DIM
    N = SSD_STATE
    halo = SUBLANES
    c = pl.program_id(0)
    hi = lax.Precision.HIGHEST

    @pl.when(c == 0)
    def _():
        xpad_ref[0:halo, :] = jnp.zeros((halo, xpad_ref.shape[1]), F32)
        bcpad_ref[0:halo, :] = jnp.zeros((halo, bcpad_ref.shape[1]), F32)
        state_ref[...] = jnp.zeros_like(state_ref)

    xpad_ref[halo:halo + L, :] = xs_ref[...]
    bcpad_ref[halo:halo + L, :] = bc_ref[...]

    def conv(pad_ref, w_ref, b_ref):
        acc = b_ref[...]
        for k in range(SSD_CONV):
            off = halo - (SSD_CONV - 1) + k
            acc = acc + pad_ref[off:off + L, :] * w_ref[k:k + 1, :]
        return _silu(acc)

    xs = conv(xpad_ref, cwx_ref, cbx_ref)
    bc = conv(bcpad_ref, cwbc_ref, cbbc_ref)
    xpad_ref[0:halo, :] = xpad_ref[L:L + halo, :]
    bcpad_ref[0:halo, :] = bcpad_ref[L:L + halo, :]

    lane = lax.broadcasted_iota(jnp.int32, (1, LANES), 1)
    wdt = jnp.concatenate([wdt_ref[...], jnp.zeros((LANES - SSD_HEADS, wdt_ref.shape[1]), F32)], axis=0)
    dtr = _dot_wt(hb_ref[...], wdt.astype(BF16)) + dtb_ref[...]
    dt = jnp.maximum(dtr, 0.0) + jnp.log1p(jnp.exp(-jnp.abs(dtr)))
    a = jnp.where(lane < SSD_HEADS, -jnp.exp(alog_ref[...]), 0.0)
    da = dt * a
    row = lax.broadcasted_iota(jnp.int32, (L, L), 0)
    col = lax.broadcasted_iota(jnp.int32, (L, L), 1)
    tril = row >= col
    a_cs = jnp.dot(tril.astype(F32), da, preferred_element_type=F32, precision=hi)
    a_cs_t = a_cs.T
    exp_a = jnp.exp(a_cs)
    decay = jnp.exp(a_cs[L - 1:L, :] - a_cs)
    stacked = jnp.concatenate([dt, exp_a, decay], axis=0)
    expand = expand_ref[...]
    wide = jnp.zeros((3 * L, SSD_WIDTH), F32)
    rest = stacked
    for _ in range(3):
        piece = rest.astype(BF16)
        wide = wide + jnp.dot(piece, expand, preferred_element_type=F32)
        rest = rest - piece.astype(F32)
    dt_w, exp_w, dec_w = wide[0:L], wide[L:2 * L], wide[2 * L:3 * L]
    xdt = xs * dt_w
    xdec_b = (xdt * dec_w).astype(BF16)
    lane_w = lax.broadcasted_iota(jnp.int32, (L, LANES), 1)
    first_head = lane_w < P

    gw = SSD_WIDTH // SSD_GROUPS
    pairs_per_group = gw // LANES
    for g in range(SSD_GROUPS):
        b_g = bc[:, g * N:(g + 1) * N]
        c_g = bc[:, SSD_GROUPS * N + g * N:SSD_GROUPS * N + (g + 1) * N].astype(BF16)
        b_t = b_g.T.astype(BF16)
        cb = jnp.dot(c_g, b_t, preferred_element_type=F32)
        ssq = jnp.zeros((L, 1), F32)
        for i in range(pairs_per_group):
            pi = g * pairs_per_group + i
            blk = slice(pi * LANES, (pi + 1) * LANES)
            x_blk = xdt[:, blk]
            y_pair = jnp.zeros((L, LANES), F32)
            for e in range(2):
                h = 2 * pi + e
                seg = jnp.exp(jnp.where(tril, a_cs[:, h:h + 1] - a_cs_t[h:h + 1, :], -jnp.inf))
                mat = (cb * seg).astype(BF16)
                keep = first_head if e == 0 else jnp.logical_not(first_head)
                y_pair = y_pair + jnp.dot(mat, jnp.where(keep, x_blk, 0.0).astype(BF16),
                                          preferred_element_type=F32)
            st = jnp.dot(b_t, xdec_b[:, blk], preferred_element_type=F32)
            h_prev = state_ref[pi]
            y_off = jnp.dot(c_g, h_prev.astype(BF16), preferred_element_type=F32) * exp_w[:, blk]
            state_ref[pi] = h_prev * exp_w[L - 1:L, blk] + st
            y = y_pair + y_off + xs[:, blk] * dskip_ref[:, blk]
            yg = y * _silu(z_ref[:, blk])
            ssq = ssq + jnp.sum(yg * yg, axis=-1, keepdims=True)
            y_ref[:, blk] = yg
        rs = lax.rsqrt(ssq / gw + RMS_EPS)
        gsl = slice(g * gw, (g + 1) * gw)
        o_ref[:, gsl] = (y_ref[:, gsl] * rs * ng_ref[:, gsl]).astype(o_ref.dtype)


def _pad_lanes(v, width=LANES):
    return jnp.pad(v.astype(F32), (0, width - v.shape[0])).reshape(1, width)


def _ssd(zx, hb, w_in_t, layer, conv_w, conv_b, dt_bias, a_log, d_skip, norm_g):
    s = zx.shape[0]
    d = hb.shape[1]
    dt_row = 3 * ATTN_WIDTH + ZX_WIDTH
    assert dt_row % SSD_HEADS == 0 and w_in_t.shape[1] == dt_row + SSD_HEADS
    L = SSD_CHUNK
    assert s % L == 0
    w = SSD_WIDTH
    expand = (jnp.arange(LANES)[:, None] == (jnp.arange(w)[None, :] // SSD_HEAD_DIM)).astype(BF16)
    d_wide = jnp.repeat(d_skip.astype(F32), SSD_HEAD_DIM).reshape(1, w)
    row1 = lambda width: pl.BlockSpec((1, width), lambda c: (0, 0))
    return pl.pallas_call(
        _ssd_kernel,
        out_shape=jax.ShapeDtypeStruct((s, w), BF16),
        grid=(s // L,),
        in_specs=[pl.BlockSpec((L, w), lambda c: (c, 0)),
                  pl.BlockSpec((L, w), lambda c: (c, 1)),
                  pl.BlockSpec((L, SSD_BC), lambda c: (c, 2 * w // SSD_BC)),
                  pl.BlockSpec((L, d), lambda c: (c, 0)),
                  pl.BlockSpec((None, SSD_HEADS, d), lambda c: (layer, dt_row // SSD_HEADS, 0)),
                  pl.BlockSpec((SSD_CONV, w), lambda c: (0, 0)),
                  pl.BlockSpec((SSD_CONV, SSD_BC), lambda c: (0, 0)),
                  row1(w), row1(SSD_BC), row1(LANES), row1(LANES), row1(w), row1(w),
                  pl.BlockSpec((LANES, w), lambda c: (0, 0))],
        out_specs=pl.BlockSpec((L, w), lambda c: (c, 0)),
        scratch_shapes=[pltpu.VMEM((L + SUBLANES, w), F32),
                        pltpu.VMEM((L + SUBLANES, SSD_BC), F32),
                        pltpu.VMEM((w // LANES, SSD_STATE, LANES), F32),
                        pltpu.VMEM((L, w), F32)],
        compiler_params=_params("arbitrary"),
        name="ssd_scan",
    )(zx, zx, zx, hb, w_in_t, conv_w[:, :w], conv_w[:, w:], conv_b[:w].reshape(1, w), conv_b[w:].reshape(1, SSD_BC),
      _pad_lanes(dt_bias), _pad_lanes(a_log), d_wide, norm_g.reshape(1, w), expand)


def _layer_norm(y, g, b):
    mu = jnp.mean(y, axis=-1, keepdims=True)
    d = y - mu
    var = jnp.mean(d * d, axis=-1, keepdims=True)
    return d * lax.rsqrt(var + LN_EPS) * g + b


def _outproj_kernel(attn_ref, ssd_ref, h_ref, w_ref, ag_ref, g_ref, b_ref, o_ref, ob_ref):
    attn = attn_ref[...]
    attn_n = attn * lax.rsqrt(jnp.mean(attn * attn, axis=-1, keepdims=True) + RMS_EPS) * ag_ref[...]
    aw = ATTN_WIDTH
    mix = jnp.dot(attn_n.astype(BF16), w_ref[0:aw, :], preferred_element_type=F32)
    mix = mix + jnp.dot(ssd_ref[...], w_ref[aw:, :], preferred_element_type=F32)
    out = _layer_norm(DEEPNORM_ALPHA * h_ref[...] + mix, g_ref[...], b_ref[...])
    o_ref[...] = out
    ob_ref[...] = out.astype(BF16)


def _outproj_ln(attn, ssd, h, w, layer, attn_g, ln_g, ln_b):
    s, d = h.shape
    tm = _tile(s, 512)
    row = lambda width: pl.BlockSpec((1, width), lambda i: (0, 0))
    return pl.pallas_call(
        _outproj_kernel,
        out_shape=(jax.ShapeDtypeStruct((s, d), F32), jax.ShapeDtypeStruct((s, d), BF16)),
        grid=(s // tm,),
        in_specs=[pl.BlockSpec((tm, ATTN_WIDTH), lambda i: (i, 0)),
                  pl.BlockSpec((tm, SSD_WIDTH), lambda i: (i, 0)),
                  pl.BlockSpec((tm, d), lambda i: (i, 0)),
                  pl.BlockSpec((None, d, d), lambda i: (layer, 0, 0), pipeline_mode=pl.Buffered(1)),
                  row(ATTN_WIDTH), row(d), row(d)],
        out_specs=(pl.BlockSpec((tm, d), lambda i: (i, 0)), pl.BlockSpec((tm, d), lambda i: (i, 0))),
        compiler_params=_params("parallel"),
        name="outproj_ln",
    )(attn, ssd, h, w, attn_g.reshape(1, -1), ln_g.reshape(1, -1), ln_b.reshape(1, -1))


def _ffn_up_kernel(x_ref, wg_ref, wu_ref, cwg_ref, cwu_ref, cbg_ref, cbu_ref, o_ref, gpad_ref, upad_ref,
                   wgb_ref, wub_ref):
    tm = x_ref.shape[0]
    halo = SUBLANES
    _cast_weights_once(wg_ref, wgb_ref)
    _cast_weights_once(wu_ref, wub_ref)

    @pl.when(pl.program_id(1) == 0)
    def _():
        gpad_ref[0:halo, :] = jnp.zeros((halo, gpad_ref.shape[1]), F32)
        upad_ref[0:halo, :] = jnp.zeros((halo, upad_ref.shape[1]), F32)

    x = x_ref[...]
    gpad_ref[halo:halo + tm, :] = jnp.dot(x, wgb_ref[...], preferred_element_type=F32)
    upad_ref[halo:halo + tm, :] = jnp.dot(x, wub_ref[...], preferred_element_type=F32)

    def conv(pad_ref, w_ref, b_ref):
        acc = b_ref[...]
        for k in range(FFN_CONV):
            off = halo - (FFN_CONV - 1) + k
            acc = acc + pad_ref[off:off + tm, :] * w_ref[k:k + 1, :]
        return acc

    g = conv(gpad_ref, cwg_ref, cbg_ref)
    u = conv(upad_ref, cwu_ref, cbu_ref)
    o_ref[...] = (_silu(g) * u).astype(o_ref.dtype)
    gpad_ref[0:halo, :] = gpad_ref[tm:tm + halo, :]
    upad_ref[0:halo, :] = upad_ref[tm:tm + halo, :]


def _ffn_up(hb, w_up, layer, conv_w, conv_b):
    s, d = hb.shape
    tm, tn = _tile(s, 1024), 512
    nj = D_FF // tn
    conv_b = conv_b.reshape(1, -1)
    return pl.pallas_call(
        _ffn_up_kernel,
        out_shape=jax.ShapeDtypeStruct((s, D_FF), BF16),
        grid=(nj, s // tm),
        in_specs=[pl.BlockSpec((tm, d), lambda j, i: (i, 0)),
                  pl.BlockSpec((None, d, tn), lambda j, i: (layer, 0, j)),
                  pl.BlockSpec((None, d, tn), lambda j, i: (layer, 0, j + nj)),
                  pl.BlockSpec((FFN_CONV, tn), lambda j, i: (0, j)),
                  pl.BlockSpec((FFN_CONV, tn), lambda j, i: (0, j + nj)),
                  pl.BlockSpec((1, tn), lambda j, i: (0, j)),
                  pl.BlockSpec((1, tn), lambda j, i: (0, j + nj))],
        out_specs=pl.BlockSpec((tm, tn), lambda j, i: (i, j)),
        scratch_shapes=[pltpu.VMEM((tm + SUBLANES, tn), F32), pltpu.VMEM((tm + SUBLANES, tn), F32),
                        pltpu.VMEM((d, tn), BF16), pltpu.VMEM((d, tn), BF16)],
        compiler_params=_params("parallel", "arbitrary"),
        name="ffn_up",
    )(hb, w_up, w_up, conv_w, conv_w, conv_b, conv_b)


def _ple_kernel(p_ref, hb_ref, wp_ref, wg_ref, o_ref, wpb_ref, wgb_ref):
    _cast_weights_once(wp_ref, wpb_ref)
    _cast_weights_once(wg_ref, wgb_ref)
    proj = jnp.dot(p_ref[...].astype(BF16), wpb_ref[...], preferred_element_type=F32)
    gate = jnp.dot(hb_ref[...], wgb_ref[...], preferred_element_type=F32)
    o_ref[...] = proj * jax.nn.sigmoid(gate)


def _ple(p, hb, w_proj, w_gate, layer):
    s, d = hb.shape
    tm, tn = _tile(s, 1024), 1024
    return pl.pallas_call(
        _ple_kernel,
        out_shape=jax.ShapeDtypeStruct((s, d), F32),
        grid=(d // tn, s // tm),
        in_specs=[pl.BlockSpec((None, None, tm, PE_DIM), lambda j, i: (layer, 0, i, 0)),
                  pl.BlockSpec((tm, d), lambda j, i: (i, 0)),
                  pl.BlockSpec((None, PE_DIM, tn), lambda j, i: (layer, 0, j)),
                  pl.BlockSpec((None, d, tn), lambda j, i: (layer, 0, j))],
        out_specs=pl.BlockSpec((tm, tn), lambda j, i: (i, j)),
        scratch_shapes=[pltpu.VMEM((PE_DIM, tn), BF16), pltpu.VMEM((d, tn), BF16)],
        compiler_params=_params("parallel", "arbitrary"),
        name="ple_gate",
    )(p, hb, w_proj, w_gate)


def _down_kernel(a_ref, w_ref, h_ref, ple_ref, g_ref, b_ref, o_ref, ob_ref):
    ffn = jnp.dot(a_ref[...], w_ref[...], preferred_element_type=F32)
    out = _layer_norm(DEEPNORM_ALPHA * h_ref[...] + ffn + ple_ref[...], g_ref[...], b_ref[...])
    o_ref[...] = out
    ob_ref[...] = out.astype(BF16)


def _down_ln(act, w_down, layer, h, ple, ln_g, ln_b):
    s, d = h.shape
    tm = _tile(s, 256)
    row = lambda width: pl.BlockSpec((1, width), lambda i: (0, 0))
    return pl.pallas_call(
        _down_kernel,
        out_shape=(jax.ShapeDtypeStruct((s, d), F32), jax.ShapeDtypeStruct((s, d), BF16)),
        grid=(s // tm,),
        in_specs=[pl.BlockSpec((tm, D_FF), lambda i: (i, 0)),
                  pl.BlockSpec((None, D_FF, d), lambda i: (layer, 0, 0), pipeline_mode=pl.Buffered(1)),
                  pl.BlockSpec((tm, d), lambda i: (i, 0)),
                  pl.BlockSpec((tm, d), lambda i: (i, 0)),
                  row(d), row(d)],
        out_specs=(pl.BlockSpec((tm, d), lambda i: (i, 0)), pl.BlockSpec((tm, d), lambda i: (i, 0))),
        compiler_params=_params("parallel"),
        name="down_ln",
    )(act, w_down, h, ple, ln_g.reshape(1, -1), ln_b.reshape(1, -1))


def kernel(x, p, positions, w_in, attn_norm_g, ssd_conv_w, ssd_conv_b, ssd_dt_bias, ssd_a_log, ssd_d, ssd_norm_g,
           w_out, ln1_g, ln1_b, w_up, ffn_conv_w, ffn_conv_b, w_down, w_pe_gate, w_pe_proj, ln2_g, ln2_b):
    b, s, d = x.shape
    assert b == 1 and d == D_MODEL
    cos, sin = _rope_tables(positions[0])
    h = x[0]
    hb = h.astype(BF16)
    w_in_t = jnp.swapaxes(w_in, 1, 2)
    w_out_b = w_out.astype(BF16)
    w_down_b = w_down.astype(BF16)
    for i in range(w_in.shape[0]):
        qkv = _qkv_proj(hb, w_in_t, i, cos, sin)
        zx = _zx_proj(hb, w_in_t, i)
        attn = _moba_attention(qkv)
        ssd = _ssd(zx, hb, w_in_t, i, ssd_conv_w[i], ssd_conv_b[i], ssd_dt_bias[i], ssd_a_log[i],
                   ssd_d[i], ssd_norm_g[i])
        h1, h1b = _outproj_ln(attn, ssd, h, w_out_b, i, attn_norm_g[i], ln1_g[i], ln1_b[i])
        act = _ffn_up(h1b, w_up, i, ffn_conv_w[i], ffn_conv_b[i])
        ple = _ple(p, h1b, w_pe_proj, w_pe_gate, i)
        h, hb = _down_ln(act, w_down_b, i, h1, ple, ln2_g[i], ln2_b[i])
    return h[None]
```

```python
import functools

import jax
import jax.numpy as jnp
from jax import lax
from jax.experimental import pallas as pl
from jax.experimental.pallas import tpu as pltpu

F32 = jnp.float32
BF16 = jnp.bfloat16

D_MODEL = 2048
DEPTH = 4
PE_DIM = 256
ATTN_HEADS = 8
ATTN_HEAD_DIM = 128
ATTN_WIDTH = ATTN_HEADS * ATTN_HEAD_DIM
MOBA_BLOCK = 256
MOBA_TOPK = 3
ROPE_DIM = ATTN_HEAD_DIM // 4
ROPE_THETA = 500000.0
SSD_WIDTH = D_MODEL - ATTN_WIDTH
SSD_HEAD_DIM = 64
SSD_HEADS = SSD_WIDTH // SSD_HEAD_DIM
SSD_GROUPS = 2
SSD_STATE = 128
SSD_CONV = 4
SSD_CHUNK = 256
SSD_BC = 2 * SSD_GROUPS * SSD_STATE
SSD_CONV_DIM = SSD_WIDTH + SSD_BC
D_FF = 5632
FFN_CONV = 3
DEEPNORM_ALPHA = (2 * DEPTH) ** 0.25
LN_EPS = 1e-5
RMS_EPS = 1e-6

LANES = 128
SUBLANES = 8
VMEM_LIMIT = 56 * 1024 * 1024

NEG_BIG = -1e30
ZX_WIDTH = SSD_WIDTH + SSD_CONV_DIM


def _params(*sem):
    return pltpu.CompilerParams(dimension_semantics=sem, vmem_limit_bytes=VMEM_LIMIT)


def _tile(n, pref):
    t = min(n, pref)
    assert n % t == 0, (n, t)
    return t


def _rope_table_kernel(pos_ref, invf_ref, sgn_ref, cos_ref, sin_ref):
    ang = pos_ref[...].astype(F32) * invf_ref[...]
    cos_ref[...] = jnp.cos(ang)
    sin_ref[...] = jnp.sin(ang) * sgn_ref[...]


def _rope_tables(positions):
    s = positions.shape[0]
    half = ROPE_DIM // 2
    inv_freq = 1.0 / (ROPE_THETA ** (jnp.arange(0, ROPE_DIM, 2, dtype=F32) / ROPE_DIM))
    zeros = jnp.zeros((LANES - ROPE_DIM,), F32)
    invf = jnp.concatenate([inv_freq, inv_freq, zeros]).reshape(1, LANES)
    sgn = jnp.concatenate([-jnp.ones((half,), F32), jnp.ones((half,), F32), zeros]).reshape(1, LANES)
    tm = _tile(s, 1024)
    return pl.pallas_call(
        _rope_table_kernel,
        out_shape=(jax.ShapeDtypeStruct((s, LANES), F32), jax.ShapeDtypeStruct((s, LANES), F32)),
        grid=(s // tm,),
        in_specs=[pl.BlockSpec((tm, 1), lambda i: (i, 0)),
                  pl.BlockSpec((1, LANES), lambda i: (0, 0)),
                  pl.BlockSpec((1, LANES), lambda i: (0, 0))],
        out_specs=(pl.BlockSpec((tm, LANES), lambda i: (i, 0)),
                   pl.BlockSpec((tm, LANES), lambda i: (i, 0))),
        compiler_params=_params("parallel"),
        name="rope_tables",
    )(positions.reshape(s, 1), invf, sgn)


PROJ_TN = 512


def _cast_weights_once(w_ref, wb_ref):
    @pl.when(pl.program_id(1) == 0)
    def _():
        wb_ref[...] = w_ref[...].astype(BF16)


def _dot_wt(x, wt):
    return lax.dot_general(x, wt, (((1,), (1,)), ((), ())), preferred_element_type=F32)


def _qkv_kernel(x_ref, wt_ref, cos_ref, sin_ref, o_ref, wb_ref, *, rope_tiles):
    _cast_weights_once(wt_ref, wb_ref)
    acc = _dot_wt(x_ref[...], wb_ref[...])
    j = pl.program_id(0)

    @pl.when(j < rope_tiles)
    def _():
        cos = cos_ref[...]
        sin = sin_ref[...]
        lane = lax.broadcasted_iota(jnp.int32, cos.shape, 1)
        half = ROPE_DIM // 2
        for c in range(acc.shape[1] // LANES):
            t = acc[:, c * LANES:(c + 1) * LANES]
            partner = jnp.where(lane < half, pltpu.roll(t, LANES - half, 1), pltpu.roll(t, half, 1))
            o_ref[:, c * LANES:(c + 1) * LANES] = (t * cos + partner * sin).astype(o_ref.dtype)

    @pl.when(j >= rope_tiles)
    def _():
        o_ref[...] = acc.astype(o_ref.dtype)


def _qkv_proj(xb, w_in_t, layer, cos, sin):
    s, k = xb.shape
    n = 3 * ATTN_WIDTH
    tm, tn = _tile(s, 2048), PROJ_TN
    return pl.pallas_call(
        functools.partial(_qkv_kernel, rope_tiles=2 * ATTN_WIDTH // tn),
        out_shape=jax.ShapeDtypeStruct((s, n), BF16),
        grid=(n // tn, s // tm),
        in_specs=[pl.BlockSpec((tm, k), lambda j, i: (i, 0)),
                  pl.BlockSpec((None, tn, k), lambda j, i: (layer, j, 0)),
                  pl.BlockSpec((tm, LANES), lambda j, i: (i, 0)),
                  pl.BlockSpec((tm, LANES), lambda j, i: (i, 0))],
        out_specs=pl.BlockSpec((tm, tn), lambda j, i: (i, j)),
        scratch_shapes=[pltpu.VMEM((tn, k), BF16)],
        compiler_params=_params("parallel", "arbitrary"),
        name="qkv_proj",
    )(xb, w_in_t, cos, sin)


def _zx_kernel(x_ref, wt_ref, o_ref, wb_ref):
    _cast_weights_once(wt_ref, wb_ref)
    o_ref[...] = _dot_wt(x_ref[...], wb_ref[...])


def _zx_proj(xb, w_in_t, layer):
    s, k = xb.shape
    n = ZX_WIDTH
    tm, tn = _tile(s, 2048), PROJ_TN
    first = 3 * ATTN_WIDTH // tn
    return pl.pallas_call(
        _zx_kernel,
        out_shape=jax.ShapeDtypeStruct((s, n), F32),
        grid=(n // tn, s // tm),
        in_specs=[pl.BlockSpec((tm, k), lambda j, i: (i, 0)),
                  pl.BlockSpec((None, tn, k), lambda j, i: (layer, first + j, 0))],
        out_specs=pl.BlockSpec((tm, tn), lambda j, i: (i, j)),
        scratch_shapes=[pltpu.VMEM((tn, k), BF16)],
        compiler_params=_params("parallel", "arbitrary"),
        name="zx_proj",
    )(xb, w_in_t)


MOBA_GROUP = 2
MOBA_HEADS_PER_STEP = 4
ONES_ROWS = 16
LOG2_E = 1.4426950408889634


def _moba_kernel(q_ref, k_ref, v_ref, o_ref, kmean_ref, vt_ref, sel_ref, s_ref, acc_ref, *, nb):
    bs = MOBA_BLOCK
    dh = ATTN_HEAD_DIM
    grp = MOBA_GROUP
    tq = grp * bs
    heads = MOBA_HEADS_PER_STEP
    qi = pl.program_id(1)
    c = dh ** -0.5 * LOG2_E
    head_cols = [slice(hh * dh, (hh + 1) * dh) for hh in range(heads)]

    @pl.when(qi == 0)
    def _():
        kmean_ref[...] = jnp.zeros_like(kmean_ref)
        vt_ref[:, :, dh:, :] = jnp.ones((heads, nb // grp, ONES_ROWS, tq), BF16)

        def body(t, carry):
            for hh in range(heads):
                cols = head_cols[hh]
                for u in range(grp):
                    n = t * grp + u
                    rows = pl.ds(pl.multiple_of(n * bs, bs), bs)
                    kmean_ref[hh, pl.ds(n, 1), :] = jnp.mean(k_ref[rows, cols].astype(F32), axis=0, keepdims=True)
                    vt_ref[hh, t, 0:dh, u * bs:(u + 1) * bs] = v_ref[rows, cols].astype(F32).T.astype(BF16)
            return carry

        lax.fori_loop(0, nb // grp, body, 0)

    q_ts = [q_ref[:, cols].astype(F32).T.astype(BF16) for cols in head_cols]

    last_group = nb // grp - 1

    def scores_into(slot, t):
        tt = jnp.minimum(t, last_group)
        keys = pl.ds(pl.multiple_of(tt * tq, tq), tq)
        for hh in range(heads):
            s_ref[slot, hh] = jnp.dot(k_ref[keys, head_cols[hh]], q_ts[hh], preferred_element_type=F32)

    gates = [jnp.dot(kmean_ref[hh].astype(BF16), q_ts[hh], preferred_element_type=F32)
             for hh in range(heads)]
    scores_into(1, qi)
    scores_into(0, 0)
    lane = lax.broadcasted_iota(jnp.int32, (1, tq), 1)
    lane_blk = lane // bs
    q_blk = qi * grp + lane_blk
    key_row = lax.broadcasted_iota(jnp.int32, (bs, tq), 0)

    m_init = []
    for hh in range(heads):
        blk = lax.broadcasted_iota(jnp.int32, gates[hh].shape, 0)
        g = jnp.where(blk < q_blk, gates[hh], -jnp.inf)
        sel = jnp.zeros(g.shape, F32)
        for _ in range(MOBA_TOPK):
            mx = jnp.max(g, axis=0, keepdims=True)
            first = jnp.min(jnp.where(g == mx, blk, nb), axis=0, keepdims=True)
            pick = blk == jnp.where(mx > -jnp.inf, first, -1)
            sel = jnp.where(pick, 1.0, sel)
            g = jnp.where(pick, -jnp.inf, g)
        sel_ref[hh] = sel

    for hh in range(heads):
        parts = []
        for u in range(grp):
            picked = sel_ref[hh, pl.ds(qi * grp + u, 1), :] > 0.0
            limit = jnp.maximum(jnp.where(lane_blk == u, lane - u * bs, -1), jnp.where(picked, bs, -1))
            parts.append(jnp.where(key_row <= limit, s_ref[1, hh, u * bs:(u + 1) * bs, :], NEG_BIG))
        mx = parts[0]
        for u in range(1, grp):
            mx = jnp.maximum(mx, parts[u])
        m0 = jnp.max(mx, axis=0, keepdims=True)
        p = jnp.concatenate([jnp.exp2((part - m0) * c).astype(BF16) for part in parts], axis=0)
        acc_ref[hh] = jnp.dot(vt_ref[hh, qi], p, preferred_element_type=F32)
        m_init.append(m0)

    def softmax_pv(slot, t, ms):
        live = t < qi
        probs, stats = [], []
        for hh in range(heads):
            parts = []
            for u in range(grp):
                picked = jnp.where(live, sel_ref[hh, pl.ds(t * grp + u, 1), :], 0.0) > 0.0
                parts.append(jnp.where(picked, s_ref[slot, hh, u * bs:(u + 1) * bs, :], NEG_BIG))
            mx = parts[0]
            for u in range(1, grp):
                mx = jnp.maximum(mx, parts[u])
            m_new = jnp.maximum(ms[hh], jnp.max(mx, axis=0, keepdims=True))
            stats.append((m_new, jnp.exp2((ms[hh] - m_new) * c)))
            probs.append(jnp.concatenate([jnp.exp2((part - m_new) * c).astype(BF16) for part in parts], axis=0))
        tt = jnp.minimum(t, last_group)
        for hh in range(heads):
            pv = jnp.dot(vt_ref[hh, tt], probs[hh], preferred_element_type=F32)
            acc_ref[hh] = stats[hh][1] * acc_ref[hh] + pv
        return tuple(m_new for m_new, _ in stats)

    def body(d, ms):
        t = 2 * d
        scores_into(1, t + 1)
        ms = softmax_pv(0, t, ms)
        scores_into(0, t + 2)
        return softmax_pv(1, t + 1, ms)

    lax.fori_loop(0, (qi + 1) // 2, body, tuple(m_init))
    for hh in range(heads):
        acc = acc_ref[hh]
        o_ref[:, hh * dh:(hh + 1) * dh] = (acc[0:dh] / acc[dh:dh + 1]).T.astype(o_ref.dtype)


def _moba_attention(qkv):
    s = qkv.shape[0]
    bs = MOBA_BLOCK
    nb = s // bs
    heads = MOBA_HEADS_PER_STEP
    assert s % bs == 0 and nb % MOBA_GROUP == 0 and ATTN_HEADS % heads == 0
    nbp = -(-nb // SUBLANES) * SUBLANES
    width = heads * ATTN_HEAD_DIM
    steps = ATTN_HEADS // heads
    tq = MOBA_GROUP * bs
    resident = lambda index_map: pl.BlockSpec((s, width), index_map, pipeline_mode=pl.Buffered(1))
    return pl.pallas_call(
        functools.partial(_moba_kernel, nb=nb),
        out_shape=jax.ShapeDtypeStruct((s, ATTN_WIDTH), F32),
        grid=(steps, s // tq),
        in_specs=[pl.BlockSpec((tq, width), lambda h, i: (i, h)),
                  resident(lambda h, i: (0, steps + h)),
                  resident(lambda h, i: (0, 2 * steps + h))],
        out_specs=pl.BlockSpec((tq, width), lambda h, i: (i, h)),
        scratch_shapes=[pltpu.VMEM((heads, nbp, ATTN_HEAD_DIM), F32),
                        pltpu.VMEM((heads, nb // MOBA_GROUP, ATTN_HEAD_DIM + ONES_ROWS, tq), BF16),
                        pltpu.VMEM((heads, nbp, tq), F32),
                        pltpu.VMEM((2, heads, tq, tq), F32),
                        pltpu.VMEM((heads, ATTN_HEAD_DIM + ONES_ROWS, tq), F32)],
        compiler_params=_params("parallel", "arbitrary"),
        name="moba_attention",
    )(qkv, qkv, qkv)


def _silu(x):
    return x * jax.nn.sigmoid(x)


def _ssd_kernel(z_ref, xs_ref, bc_ref, hb_ref, wdt_ref, cwx_ref, cwbc_ref, cbx_ref, cbbc_ref, dtb_ref, alog_ref,
                dskip_ref, ng_ref, expand_ref, o_ref, xpad_ref, bcpad_ref, state_ref, y_ref):
    L = SSD_CHUNK
    P = SSD_HEAD_DIM
    N = SSD_STATE
    halo = SUBLANES
    c = pl.program_id(0)
    hi = lax.Precision.HIGHEST

    @pl.when(c == 0)
    def _():
        xpad_ref[0:halo, :] = jnp.zeros((halo, xpad_ref.shape[1]), F32)
        bcpad_ref[0:halo, :] = jnp.zeros((halo, bcpad_ref.shape[1]), F32)
        state_ref[...] = jnp.zeros_like(state_ref)

    xpad_ref[halo:halo + L, :] = xs_ref[...]
    bcpad_ref[halo:halo + L, :] = bc_ref[...]

    def conv(pad_ref, w_ref, b_ref):
        acc = b_ref[...]
        for k in range(SSD_CONV):
            off = halo - (SSD_CONV - 1) + k
            acc = acc + pad_ref[off:off + L, :] * w_ref[k:k + 1, :]
        return _silu(acc)

    xs = conv(xpad_ref, cwx_ref, cbx_ref)
    bc = conv(bcpad_ref, cwbc_ref, cbbc_ref)
    xpad_ref[0:halo, :] = xpad_ref[L:L + halo, :]
    bcpad_ref[0:halo, :] = bcpad_ref[L:L + halo, :]

    lane = lax.broadcasted_iota(jnp.int32, (1, LANES), 1)
    wdt = jnp.concatenate([wdt_ref[...], jnp.zeros((LANES - SSD_HEADS, wdt_ref.shape[1]), F32)], axis=0)
    dtr = _dot_wt(hb_ref[...], wdt.astype(BF16)) + dtb_ref[...]
    dt = jnp.maximum(dtr, 0.0) + jnp.log1p(jnp.exp(-jnp.abs(dtr)))
    a = jnp.where(lane < SSD_HEADS, -jnp.exp(alog_ref[...]), 0.0)
    da = dt * a
    row = lax.broadcasted_iota(jnp.int32, (L, L), 0)
    col = lax.broadcasted_iota(jnp.int32, (L, L), 1)
    tril = row >= col
    a_cs = jnp.dot(tril.astype(F32), da, preferred_element_type=F32, precision=hi)
    a_cs_t = a_cs.T
    exp_a = jnp.exp(a_cs)
    decay = jnp.exp(a_cs[L - 1:L, :] - a_cs)
    stacked = jnp.concatenate([dt, exp_a, decay], axis=0)
    expand = expand_ref[...]
    wide = jnp.zeros((3 * L, SSD_WIDTH), F32)
    rest = stacked
    for _ in range(3):
        piece = rest.astype(BF16)
        wide = wide + jnp.dot(piece, expand, preferred_element_type=F32)
        rest = rest - piece.astype(F32)
    dt_w, exp_w, dec_w = wide[0:L], wide[L:2 * L], wide[2 * L:3 * L]
    xdt = xs * dt_w
    xdec_b = (xdt * dec_w).astype(BF16)
    lane_w = lax.broadcasted_iota(jnp.int32, (L, LANES), 1)
    first_head = lane_w < P

    gw = SSD_WIDTH // SSD_GROUPS
    pairs_per_group = gw // LANES
    for g in range(SSD_GROUPS):
        b_g = bc[:, g * N:(g + 1) * N]
        c_g = bc[:, SSD_GROUPS * N + g * N:SSD_GROUPS * N + (g + 1) * N].astype(BF16)
        b_t = b_g.T.astype(BF16)
        cb = jnp.dot(c_g, b_t, preferred_element_type=F32)
        ssq = jnp.zeros((L, 1), F32)
        for i in range(pairs_per_group):
            pi = g * pairs_per_group + i
            blk = slice(pi * LANES, (pi + 1) * LANES)
            x_blk = xdt[:, blk]
            y_pair = jnp.zeros((L, LANES), F32)
            for e in range(2):
                h = 2 * pi + e
                seg = jnp.exp(jnp.where(tril, a_cs[:, h:h + 1] - a_cs_t[h:h + 1, :], -jnp.inf))
                mat = (cb * seg).astype(BF16)
                keep = first_head if e == 0 else jnp.logical_not(first_head)
                y_pair = y_pair + jnp.dot(mat, jnp.where(keep, x_blk, 0.0).astype(BF16),
                                          preferred_element_type=F32)
            st = jnp.dot(b_t, xdec_b[:, blk], preferred_element_type=F32)
            h_prev = state_ref[pi]
            y_off = jnp.dot(c_g, h_prev.astype(BF16), preferred_element_type=F32) * exp_w[:, blk]
            state_ref[pi] = h_prev * exp_w[L - 1:L, blk] + st
            y = y_pair + y_off + xs[:, blk] * dskip_ref[:, blk]
            yg = y * _silu(z_ref[:, blk])
            ssq = ssq + jnp.sum(yg * yg, axis=-1, keepdims=True)
            y_ref[:, blk] = yg
        rs = lax.rsqrt(ssq / gw + RMS_EPS)
        gsl = slice(g * gw, (g + 1) * gw)
        o_ref[:, gsl] = (y_ref[:, gsl] * rs * ng_ref[:, gsl]).astype(o_ref.dtype)


def _pad_lanes(v, width=LANES):
    return jnp.pad(v.astype(F32), (0, width - v.shape[0])).reshape(1, width)


def _ssd(zx, hb, w_in_t, layer, conv_w, conv_b, dt_bias, a_log, d_skip, norm_g):
    s = zx.shape[0]
    d = hb.shape[1]
    dt_row = 3 * ATTN_WIDTH + ZX_WIDTH
    assert dt_row % SSD_HEADS == 0 and w_in_t.shape[1] == dt_row + SSD_HEADS
    L = SSD_CHUNK
    assert s % L == 0
    w = SSD_WIDTH
    expand = (jnp.arange(LANES)[:, None] == (jnp.arange(w)[None, :] // SSD_HEAD_DIM)).astype(BF16)
    d_wide = jnp.repeat(d_skip.astype(F32), SSD_HEAD_DIM).reshape(1, w)
    row1 = lambda width: pl.BlockSpec((1, width), lambda c: (0, 0))
    return pl.pallas_call(
        _ssd_kernel,
        out_shape=jax.ShapeDtypeStruct((s, w), BF16),
        grid=(s // L,),
        in_specs=[pl.BlockSpec((L, w), lambda c: (c, 0)),
                  pl.BlockSpec((L, w), lambda c: (c, 1)),
                  pl.BlockSpec((L, SSD_BC), lambda c: (c, 2 * w // SSD_BC)),
                  pl.BlockSpec((L, d), lambda c: (c, 0)),
                  pl.BlockSpec((None, SSD_HEADS, d), lambda c: (layer, dt_row // SSD_HEADS, 0)),
                  pl.BlockSpec((SSD_CONV, w), lambda c: (0, 0)),
                  pl.BlockSpec((SSD_CONV, SSD_BC), lambda c: (0, 0)),
                  row1(w), row1(SSD_BC), row1(LANES), row1(LANES), row1(w), row1(w),
                  pl.BlockSpec((LANES, w), lambda c: (0, 0))],
        out_specs=pl.BlockSpec((L, w), lambda c: (c, 0)),
        scratch_shapes=[pltpu.VMEM((L + SUBLANES, w), F32),
                        pltpu.VMEM((L + SUBLANES, SSD_BC), F32),
                        pltpu.VMEM((w // LANES, SSD_STATE, LANES), F32),
                        pltpu.VMEM((L, w), F32)],
        compiler_params=_params("arbitrary"),
        name="ssd_scan",
    )(zx, zx, zx, hb, w_in_t, conv_w[:, :w], conv_w[:, w:], conv_b[:w].reshape(1, w), conv_b[w:].reshape(1, SSD_BC),
      _pad_lanes(dt_bias), _pad_lanes(a_log), d_wide, norm_g.reshape(1, w), expand)


def _layer_norm(y, g, b):
    mu = jnp.mean(y, axis=-1, keepdims=True)
    d = y - mu
    var = jnp.mean(d * d, axis=-1, keepdims=True)
    return d * lax.rsqrt(var + LN_EPS) * g + b


def _outproj_kernel(attn_ref, ssd_ref, h_ref, w_ref, ag_ref, g_ref, b_ref, o_ref, ob_ref):
    attn = attn_ref[...]
    attn_n = attn * lax.rsqrt(jnp.mean(attn * attn, axis=-1, keepdims=True) + RMS_EPS) * ag_ref[...]
    aw = ATTN_WIDTH
    mix = jnp.dot(attn_n.astype(BF16), w_ref[0:aw, :], preferred_element_type=F32)
    mix = mix + jnp.dot(ssd_ref[...], w_ref[aw:, :], preferred_element_type=F32)
    out = _layer_norm(DEEPNORM_ALPHA * h_ref[...] + mix, g_ref[...], b_ref[...])
    o_ref[...] = out
    ob_ref[...] = out.astype(BF16)


def _outproj_ln(attn, ssd, h, w, layer, attn_g, ln_g, ln_b):
    s, d = h.shape
    tm = _tile(s, 512)
    row = lambda width: pl.BlockSpec((1, width), lambda i: (0, 0))
    return pl.pallas_call(
        _outproj_kernel,
        out_shape=(jax.ShapeDtypeStruct((s, d), F32), jax.ShapeDtypeStruct((s, d), BF16)),
        grid=(s // tm,),
        in_specs=[pl.BlockSpec((tm, ATTN_WIDTH), lambda i: (i, 0)),
                  pl.BlockSpec((tm, SSD_WIDTH), lambda i: (i, 0)),
                  pl.BlockSpec((tm, d), lambda i: (i, 0)),
                  pl.BlockSpec((None, d, d), lambda i: (layer, 0, 0), pipeline_mode=pl.Buffered(1)),
                  row(ATTN_WIDTH), row(d), row(d)],
        out_specs=(pl.BlockSpec((tm, d), lambda i: (i, 0)), pl.BlockSpec((tm, d), lambda i: (i, 0))),
        compiler_params=_params("parallel"),
        name="outproj_ln",
    )(attn, ssd, h, w, attn_g.reshape(1, -1), ln_g.reshape(1, -1), ln_b.reshape(1, -1))


def _ffn_up_kernel(x_ref, wg_ref, wu_ref, cwg_ref, cwu_ref, cbg_ref, cbu_ref, o_ref, gpad_ref, upad_ref,
                   wgb_ref, wub_ref):
    tm = x_ref.shape[0]
    halo = SUBLANES
    _cast_weights_once(wg_ref, wgb_ref)
    _cast_weights_once(wu_ref, wub_ref)

    @pl.when(pl.program_id(1) == 0)
    def _():
        gpad_ref[0:halo, :] = jnp.zeros((halo, gpad_ref.shape[1]), F32)
        upad_ref[0:halo, :] = jnp.zeros((halo, upad_ref.shape[1]), F32)

    x = x_ref[...]
    gpad_ref[halo:halo + tm, :] = jnp.dot(x, wgb_ref[...], preferred_element_type=F32)
    upad_ref[halo:halo + tm, :] = jnp.dot(x, wub_ref[...], preferred_element_type=F32)

    def conv(pad_ref, w_ref, b_ref):
        acc = b_ref[...]
        for k in range(FFN_CONV):
            off = halo - (FFN_CONV - 1) + k
            acc = acc + pad_ref[off:off + tm, :] * w_ref[k:k + 1, :]
        return acc

    g = conv(gpad_ref, cwg_ref, cbg_ref)
    u = conv(upad_ref, cwu_ref, cbu_ref)
    o_ref[...] = (_silu(g) * u).astype(o_ref.dtype)
    gpad_ref[0:halo, :] = gpad_ref[tm:tm + halo, :]
    upad_ref[0:halo, :] = upad_ref[tm:tm + halo, :]


def _ffn_up(hb, w_up, layer, conv_w, conv_b):
    s, d = hb.shape
    tm, tn = _tile(s, 1024), 512
    nj = D_FF // tn
    conv_b = conv_b.reshape(1, -1)
    return pl.pallas_call(
        _ffn_up_kernel,
        out_shape=jax.ShapeDtypeStruct((s, D_FF), BF16),
        grid=(nj, s // tm),
        in_specs=[pl.BlockSpec((tm, d), lambda j, i: (i, 0)),
                  pl.BlockSpec((None, d, tn), lambda j, i: (layer, 0, j)),
                  pl.BlockSpec((None, d, tn), lambda j, i: (layer, 0, j + nj)),
                  pl.BlockSpec((FFN_CONV, tn), lambda j, i: (0, j)),
                  pl.BlockSpec((FFN_CONV, tn), lambda j, i: (0, j + nj)),
                  pl.BlockSpec((1, tn), lambda j, i: (0, j)),
                  pl.BlockSpec((1, tn), lambda j, i: (0, j + nj))],
        out_specs=pl.BlockSpec((tm, tn), lambda j, i: (i, j)),
        scratch_shapes=[pltpu.VMEM((tm + SUBLANES, tn), F32), pltpu.VMEM((tm + SUBLANES, tn), F32),
                        pltpu.VMEM((d, tn), BF16), pltpu.VMEM((d, tn), BF16)],
        compiler_params=_params("parallel", "arbitrary"),
        name="ffn_up",
    )(hb, w_up, w_up, conv_w, conv_w, conv_b, conv_b)


def _ple_kernel(p_ref, hb_ref, wp_ref, wg_ref, o_ref, wpb_ref, wgb_ref):
    _cast_weights_once(wp_ref, wpb_ref)
    _cast_weights_once(wg_ref, wgb_ref)
    proj = jnp.dot(p_ref[...].astype(BF16), wpb_ref[...], preferred_element_type=F32)
    gate = jnp.dot(hb_ref[...], wgb_ref[...], preferred_element_type=F32)
    o_ref[...] = proj * jax.nn.sigmoid(gate)


def _ple(p, hb, w_proj, w_gate, layer):
    s, d = hb.shape
    tm, tn = _tile(s, 1024), 1024
    return pl.pallas_call(
        _ple_kernel,
        out_shape=jax.ShapeDtypeStruct((s, d), F32),
        grid=(d // tn, s // tm),
        in_specs=[pl.BlockSpec((None, None, tm, PE_DIM), lambda j, i: (layer, 0, i, 0)),
                  pl.BlockSpec((tm, d), lambda j, i: (i, 0)),
                  pl.BlockSpec((None, PE_DIM, tn), lambda j, i: (layer, 0, j)),
                  pl.BlockSpec((None, d, tn), lambda j, i: (layer, 0, j))],
        out_specs=pl.BlockSpec((tm, tn), lambda j, i: (i, j)),
        scratch_shapes=[pltpu.VMEM((PE_DIM, tn), BF16), pltpu.VMEM((d, tn), BF16)],
        compiler_params=_params("parallel", "arbitrary"),
        name="ple_gate",
    )(p, hb, w_proj, w_gate)


def _down_kernel(a_ref, w_ref, h_ref, ple_ref, g_ref, b_ref, o_ref, ob_ref):
    ffn = jnp.dot(a_ref[...], w_ref[...], preferred_element_type=F32)
    out = _layer_norm(DEEPNORM_ALPHA * h_ref[...] + ffn + ple_ref[...], g_ref[...], b_ref[...])
    o_ref[...] = out
    ob_ref[...] = out.astype(BF16)


def _down_ln(act, w_down, layer, h, ple, ln_g, ln_b):
    s, d = h.shape
    tm = _tile(s, 256)
    row = lambda width: pl.BlockSpec((1, width), lambda i: (0, 0))
    return pl.pallas_call(
        _down_kernel,
        out_shape=(jax.ShapeDtypeStruct((s, d), F32), jax.ShapeDtypeStruct((s, d), BF16)),
        grid=(s // tm,),
        in_specs=[pl.BlockSpec((tm, D_FF), lambda i: (i, 0)),
                  pl.BlockSpec((None, D_FF, d), lambda i: (layer, 0, 0), pipeline_mode=pl.Buffered(1)),
                  pl.BlockSpec((tm, d), lambda i: (i, 0)),
                  pl.BlockSpec((tm, d), lambda i: (i, 0)),
                  row(d), row(d)],
        out_specs=(pl.BlockSpec((tm, d), lambda i: (i, 0)), pl.BlockSpec((tm, d), lambda i: (i, 0))),
        compiler_params=_params("parallel"),
        name="down_ln",
    )(act, w_down, h, ple, ln_g.reshape(1, -1), ln_b.reshape(1, -1))


def kernel(x, p, positions, w_in, attn_norm_g, ssd_conv_w, ssd_conv_b, ssd_dt_bias, ssd_a_log, ssd_d, ssd_norm_g,
           w_out, ln1_g, ln1_b, w_up, ffn_conv_w, ffn_conv_b, w_down, w_pe_gate, w_pe_proj, ln2_g, ln2_b):
    b, s, d = x.shape
    assert b == 1 and d == D_MODEL
    cos, sin = _rope_tables(positions[0])
    h = x[0]
    hb = h.astype(BF16)
    w_in_t = jnp.swapaxes(w_in, 1, 2)
    w_out_b = w_out.astype(BF16)
    w_down_b = w_down.astype(BF16)
    for i in range(w_in.shape[0]):
        qkv = _qkv_proj(hb, w_in_t, i, cos, sin)
        zx = _zx_proj(hb, w_in_t, i)
        attn = _moba_attention(qkv)
        ssd = _ssd(zx, hb, w_in_t, i, ssd_conv_w[i], ssd_conv_b[i], ssd_dt_bias[i], ssd_a_log[i],
                   ssd_d[i], ssd_norm_g[i])
        h1, h1b = _outproj_ln(attn, ssd, h, w_out_b, i, attn_norm_g[i], ln1_g[i], ln1_b[i])
        act = _ffn_up(h1b, w_up, i, ffn_conv_w[i], ffn_conv_b[i])
        ple = _ple(p, h1b, w_pe_proj, w_pe_gate, i)
        h, hb = _down_ln(act, w_down_b, i, h1, ple, ln2_g[i], ln2_b[i])
    return h[None]
```

```python
import functools

import jax
import jax.numpy as jnp
from jax import lax
from jax.experimental import pallas as pl
from jax.experimental.pallas import tpu as pltpu

F32 = jnp.float32
BF16 = jnp.bfloat16

D_MODEL = 2048
DEPTH = 4
PE_DIM = 256
ATTN_HEADS = 8
ATTN_HEAD_DIM = 128
ATTN_WIDTH = ATTN_HEADS * ATTN_HEAD_DIM
MOBA_BLOCK = 256
MOBA_TOPK = 3
ROPE_DIM = ATTN_HEAD_DIM // 4
ROPE_THETA = 500000.0
SSD_WIDTH = D_MODEL - ATTN_WIDTH
SSD_HEAD_DIM = 64
SSD_HEADS = SSD_WIDTH // SSD_HEAD_DIM
SSD_GROUPS = 2
SSD_STATE = 128
SSD_CONV = 4
SSD_CHUNK = 256
SSD_BC = 2 * SSD_GROUPS * SSD_STATE
SSD_CONV_DIM = SSD_WIDTH + SSD_BC
D_FF = 5632
FFN_CONV = 3
DEEPNORM_ALPHA = (2 * DEPTH) ** 0.25
LN_EPS = 1e-5
RMS_EPS = 1e-6

LANES = 128
SUBLANES = 8
VMEM_LIMIT = 56 * 1024 * 1024

NEG_BIG = -1e30
ZX_WIDTH = SSD_WIDTH + SSD_CONV_DIM


def _params(*sem):
    return pltpu.CompilerParams(dimension_semantics=sem, vmem_limit_bytes=VMEM_LIMIT)


def _tile(n, pref):
    t = min(n, pref)
    assert n % t == 0, (n, t)
    return t


def _rope_table_kernel(pos_ref, invf_ref, sgn_ref, cos_ref, sin_ref):
    ang = pos_ref[...].astype(F32) * invf_ref[...]
    cos_ref[...] = jnp.cos(ang)
    sin_ref[...] = jnp.sin(ang) * sgn_ref[...]


def _rope_tables(positions):
    s = positions.shape[0]
    half = ROPE_DIM // 2
    inv_freq = 1.0 / (ROPE_THETA ** (jnp.arange(0, ROPE_DIM, 2, dtype=F32) / ROPE_DIM))
    zeros = jnp.zeros((LANES - ROPE_DIM,), F32)
    invf = jnp.concatenate([inv_freq, inv_freq, zeros]).reshape(1, LANES)
    sgn = jnp.concatenate([-jnp.ones((half,), F32), jnp.ones((half,), F32), zeros]).reshape(1, LANES)
    tm = _tile(s, 1024)
    return pl.pallas_call(
        _rope_table_kernel,
        out_shape=(jax.ShapeDtypeStruct((s, LANES), F32), jax.ShapeDtypeStruct((s, LANES), F32)),
        grid=(s // tm,),
        in_specs=[pl.BlockSpec((tm, 1), lambda i: (i, 0)),
                  pl.BlockSpec((1, LANES), lambda i: (0, 0)),
                  pl.BlockSpec((1, LANES), lambda i: (0, 0))],
        out_specs=(pl.BlockSpec((tm, LANES), lambda i: (i, 0)),
                   pl.BlockSpec((tm, LANES), lambda i: (i, 0))),
        compiler_params=_params("parallel"),
        name="rope_tables",
    )(positions.reshape(s, 1), invf, sgn)


PROJ_TN = 512


def _cast_weights_once(w_ref, wb_ref):
    @pl.when(pl.program_id(1) == 0)
    def _():
        wb_ref[...] = w_ref[...].astype(BF16)


def _dot_wt(x, wt):
    return lax.dot_general(x, wt, (((1,), (1,)), ((), ())), preferred_element_type=F32)


def _qkv_kernel(x_ref, wt_ref, cos_ref, sin_ref, o_ref, wb_ref, *, rope_tiles):
    _cast_weights_once(wt_ref, wb_ref)
    acc = _dot_wt(x_ref[...], wb_ref[...])
    j = pl.program_id(0)

    @pl.when(j < rope_tiles)
    def _():
        cos = cos_ref[...]
        sin = sin_ref[...]
        lane = lax.broadcasted_iota(jnp.int32, cos.shape, 1)
        half = ROPE_DIM // 2
        for c in range(acc.shape[1] // LANES):
            t = acc[:, c * LANES:(c + 1) * LANES]
            partner = jnp.where(lane < half, pltpu.roll(t, LANES - half, 1), pltpu.roll(t, half, 1))
            o_ref[:, c * LANES:(c + 1) * LANES] = (t * cos + partner * sin).astype(o_ref.dtype)

    @pl.when(j >= rope_tiles)
    def _():
        o_ref[...] = acc.astype(o_ref.dtype)


def _qkv_proj(xb, w_in_t, layer, cos, sin):
    s, k = xb.shape
    n = 3 * ATTN_WIDTH
    tm, tn = _tile(s, 2048), PROJ_TN
    return pl.pallas_call(
        functools.partial(_qkv_kernel, rope_tiles=2 * ATTN_WIDTH // tn),
        out_shape=jax.ShapeDtypeStruct((s, n), BF16),
        grid=(n // tn, s // tm),
        in_specs=[pl.BlockSpec((tm, k), lambda j, i: (i, 0)),
                  pl.BlockSpec((None, tn, k), lambda j, i: (layer, j, 0)),
                  pl.BlockSpec((tm, LANES), lambda j, i: (i, 0)),
                  pl.BlockSpec((tm, LANES), lambda j, i: (i, 0))],
        out_specs=pl.BlockSpec((tm, tn), lambda j, i: (i, j)),
        scratch_shapes=[pltpu.VMEM((tn, k), BF16)],
        compiler_params=_params("parallel", "arbitrary"),
        name="qkv_proj",
    )(xb, w_in_t, cos, sin)


def _zx_kernel(x_ref, wt_ref, o_ref, wb_ref):
    _cast_weights_once(wt_ref, wb_ref)
    o_ref[...] = _dot_wt(x_ref[...], wb_ref[...])


def _zx_proj(xb, w_in_t, layer):
    s, k = xb.shape
    n = ZX_WIDTH
    tm, tn = _tile(s, 2048), PROJ_TN
    first = 3 * ATTN_WIDTH // tn
    return pl.pallas_call(
        _zx_kernel,
        out_shape=jax.ShapeDtypeStruct((s, n), F32),
        grid=(n // tn, s // tm),
        in_specs=[pl.BlockSpec((tm, k), lambda j, i: (i, 0)),
                  pl.BlockSpec((None, tn, k), lambda j, i: (layer, first + j, 0))],
        out_specs=pl.BlockSpec((tm, tn), lambda j, i: (i, j)),
        scratch_shapes=[pltpu.VMEM((tn, k), BF16)],
        compiler_params=_params("parallel", "arbitrary"),
        name="zx_proj",
    )(xb, w_in_t)


MOBA_GROUP = 2
MOBA_HEADS_PER_STEP = 4
ONES_ROWS = 16
LOG2_E = 1.4426950408889634


def _moba_kernel(q_ref, k_ref, v_ref, o_ref, kmean_ref, vt_ref, sel_ref, s_ref, acc_ref, *, nb):
    bs = MOBA_BLOCK
    dh = ATTN_HEAD_DIM
    grp = MOBA_GROUP
    tq = grp * bs
    heads = MOBA_HEADS_PER_STEP
    qi = pl.program_id(1)
    c = dh ** -0.5 * LOG2_E
    head_cols = [slice(hh * dh, (hh + 1) * dh) for hh in range(heads)]

    @pl.when(qi == 0)
    def _():
        kmean_ref[...] = jnp.zeros_like(kmean_ref)
        vt_ref[:, :, dh:, :] = jnp.ones((heads, nb // grp, ONES_ROWS, tq), BF16)

        def body(t, carry):
            for hh in range(heads):
                cols = head_cols[hh]
                for u in range(grp):
                    n = t * grp + u
                    rows = pl.ds(pl.multiple_of(n * bs, bs), bs)
                    kmean_ref[hh, pl.ds(n, 1), :] = jnp.mean(k_ref[rows, cols].astype(F32), axis=0, keepdims=True)
                    vt_ref[hh, t, 0:dh, u * bs:(u + 1) * bs] = v_ref[rows, cols].astype(F32).T.astype(BF16)
            return carry

        lax.fori_loop(0, nb // grp, body, 0)

    q_ts = [q_ref[:, cols].astype(F32).T.astype(BF16) for cols in head_cols]

    last_group = nb // grp - 1

    def scores_into(slot, t):
        tt = jnp.minimum(t, last_group)
        keys = pl.ds(pl.multiple_of(tt * tq, tq), tq)
        for hh in range(heads):
            s_ref[slot, hh] = jnp.dot(k_ref[keys, head_cols[hh]], q_ts[hh], preferred_element_type=F32)

    gates = [jnp.dot(kmean_ref[hh].astype(BF16), q_ts[hh], preferred_element_type=F32)
             for hh in range(heads)]
    scores_into(1, qi)
    scores_into(0, 0)
    lane = lax.broadcasted_iota(jnp.int32, (1, tq), 1)
    lane_blk = lane // bs
    q_blk = qi * grp + lane_blk
    key_row = lax.broadcasted_iota(jnp.int32, (bs, tq), 0)

    m_init = []
    for hh in range(heads):
        blk = lax.broadcasted_iota(jnp.int32, gates[hh].shape, 0)
        g = jnp.where(blk < q_blk, gates[hh], -jnp.inf)
        sel = jnp.zeros(g.shape, F32)
        for _ in range(MOBA_TOPK):
            mx = jnp.max(g, axis=0, keepdims=True)
            first = jnp.min(jnp.where(g == mx, blk, nb), axis=0, keepdims=True)
            pick = blk == jnp.where(mx > -jnp.inf, first, -1)
            sel = jnp.where(pick, 1.0, sel)
            g = jnp.where(pick, -jnp.inf, g)
        sel_ref[hh] = sel

    for hh in range(heads):
        parts = []
        for u in range(grp):
            picked = sel_ref[hh, pl.ds(qi * grp + u, 1), :] > 0.0
            limit = jnp.maximum(jnp.where(lane_blk == u, lane - u * bs, -1), jnp.where(picked, bs, -1))
            parts.append(jnp.where(key_row <= limit, s_ref[1, hh, u * bs:(u + 1) * bs, :], NEG_BIG))
        mx = parts[0]
        for u in range(1, grp):
            mx = jnp.maximum(mx, parts[u])
        m0 = jnp.max(mx, axis=0, keepdims=True)
        p = jnp.concatenate([jnp.exp2((part - m0) * c).astype(BF16) for part in parts], axis=0)
        acc_ref[hh] = jnp.dot(vt_ref[hh, qi], p, preferred_element_type=F32)
        m_init.append(m0)

    def softmax_pv(slot, t, ms):
        live = t < qi
        probs, stats = [], []
        for hh in range(heads):
            parts = []
            for u in range(grp):
                picked = jnp.where(live, sel_ref[hh, pl.ds(t * grp + u, 1), :], 0.0) > 0.0
                parts.append(jnp.where(picked, s_ref[slot, hh, u * bs:(u + 1) * bs, :], NEG_BIG))
            mx = parts[0]
            for u in range(1, grp):
                mx = jnp.maximum(mx, parts[u])
            m_new = jnp.maximum(ms[hh], jnp.max(mx, axis=0, keepdims=True))
            stats.append((m_new, jnp.exp2((ms[hh] - m_new) * c)))
            probs.append(jnp.concatenate([jnp.exp2((part - m_new) * c).astype(BF16) for part in parts], axis=0))
        tt = jnp.minimum(t, last_group)
        for hh in range(heads):
            pv = jnp.dot(vt_ref[hh, tt], probs[hh], preferred_element_type=F32)
            acc_ref[hh] = stats[hh][1] * acc_ref[hh] + pv
        return tuple(m_new for m_new, _ in stats)

    def body(d, ms):
        t = 2 * d
        scores_into(1, t + 1)
        ms = softmax_pv(0, t, ms)
        scores_into(0, t + 2)
        return softmax_pv(1, t + 1, ms)

    lax.fori_loop(0, (qi + 1) // 2, body, tuple(m_init))
    for hh in range(heads):
        acc = acc_ref[hh]
        o_ref[:, hh * dh:(hh + 1) * dh] = (acc[0:dh] / acc[dh:dh + 1]).T.astype(o_ref.dtype)


def _moba_attention(qkv):
    s = qkv.shape[0]
    bs = MOBA_BLOCK
    nb = s // bs
    heads = MOBA_HEADS_PER_STEP
    assert s % bs == 0 and nb % MOBA_GROUP == 0 and ATTN_HEADS % heads == 0
    nbp = -(-nb // SUBLANES) * SUBLANES
    width = heads * ATTN_HEAD_DIM
    steps = ATTN_HEADS // heads
    tq = MOBA_GROUP * bs
    resident = lambda index_map: pl.BlockSpec((s, width), index_map, pipeline_mode=pl.Buffered(1))
    return pl.pallas_call(
        functools.partial(_moba_kernel, nb=nb),
        out_shape=jax.ShapeDtypeStruct((s, ATTN_WIDTH), F32),
        grid=(steps, s // tq),
        in_specs=[pl.BlockSpec((tq, width), lambda h, i: (i, h)),
                  resident(lambda h, i: (0, steps + h)),
                  resident(lambda h, i: (0, 2 * steps + h))],
        out_specs=pl.BlockSpec((tq, width), lambda h, i: (i, h)),
        scratch_shapes=[pltpu.VMEM((heads, nbp, ATTN_HEAD_DIM), F32),
                        pltpu.VMEM((heads, nb // MOBA_GROUP, ATTN_HEAD_DIM + ONES_ROWS, tq), BF16),
                        pltpu.VMEM((heads, nbp, tq), F32),
                        pltpu.VMEM((2, heads, tq, tq), F32),
                        pltpu.VMEM((heads, ATTN_HEAD_DIM + ONES_ROWS, tq), F32)],
        compiler_params=_params("parallel", "arbitrary"),
        name="moba_attention",
    )(qkv, qkv, qkv)


def _silu(x):
    return x * jax.nn.sigmoid(x)


def _ssd_kernel(z_ref, xs_ref, bc_ref, hb_ref, wdt_ref, cwx_ref, cwbc_ref, cbx_ref, cbbc_ref, dtb_ref, alog_ref,
                dskip_ref, ng_ref, expand_ref, o_ref, xpad_ref, bcpad_ref, state_ref, y_ref):
    L = SSD_CHUNK
    P = SSD_HEAD_DIM
    N = SSD_STATE
    halo = SUBLANES
    c = pl.program_id(0)
    hi = lax.Precision.HIGHEST

    @pl.when(c == 0)
    def _():
        xpad_ref[0:halo, :] = jnp.zeros((halo, xpad_ref.shape[1]), F32)
        bcpad_ref[0:halo, :] = jnp.zeros((halo, bcpad_ref.shape[1]), F32)
        state_ref[...] = jnp.zeros_like(state_ref)

    xpad_ref[halo:halo + L, :] = xs_ref[...]
    bcpad_ref[halo:halo + L, :] = bc_ref[...]

    def conv(pad_ref, w_ref, b_ref):
        acc = b_ref[...]
        for k in range(SSD_CONV):
            off = halo - (SSD_CONV - 1) + k
            acc = acc + pad_ref[off:off + L, :] * w_ref[k:k + 1, :]
        return _silu(acc)

    xs = conv(xpad_ref, cwx_ref, cbx_ref)
    bc = conv(bcpad_ref, cwbc_ref, cbbc_ref)
    xpad_ref[0:halo, :] = xpad_ref[L:L + halo, :]
    bcpad_ref[0:halo, :] = bcpad_ref[L:L + halo, :]

    lane = lax.broadcasted_iota(jnp.int32, (1, LANES), 1)
    wdt = jnp.concatenate([wdt_ref[...], jnp.zeros((LANES - SSD_HEADS, wdt_ref.shape[1]), F32)], axis=0)
    dtr = _dot_wt(hb_ref[...], wdt.astype(BF16)) + dtb_ref[...]
    dt = jnp.maximum(dtr, 0.0) + jnp.log1p(jnp.exp(-jnp.abs(dtr)))
    a = jnp.where(lane < SSD_HEADS, -jnp.exp(alog_ref[...]), 0.0)
    da = dt * a
    row = lax.broadcasted_iota(jnp.int32, (L, L), 0)
    col = lax.broadcasted_iota(jnp.int32, (L, L), 1)
    tril = row >= col
    a_cs = jnp.dot(tril.astype(F32), da, preferred_element_type=F32, precision=hi)
    a_cs_t = a_cs.T
    exp_a = jnp.exp(a_cs)
    decay = jnp.exp(a_cs[L - 1:L, :] - a_cs)
    stacked = jnp.concatenate([dt, exp_a, decay], axis=0)
    expand = expand_ref[...]
    wide = jnp.zeros((3 * L, SSD_WIDTH), F32)
    rest = stacked
    for _ in range(3):
        piece = rest.astype(BF16)
        wide = wide + jnp.dot(piece, expand, preferred_element_type=F32)
        rest = rest - piece.astype(F32)
    dt_w, exp_w, dec_w = wide[0:L], wide[L:2 * L], wide[2 * L:3 * L]
    xdt = xs * dt_w
    xdec_b = (xdt * dec_w).astype(BF16)
    lane_w = lax.broadcasted_iota(jnp.int32, (L, LANES), 1)
    first_head = lane_w < P

    gw = SSD_WIDTH // SSD_GROUPS
    pairs_per_group = gw // LANES
    for g in range(SSD_GROUPS):
        b_g = bc[:, g * N:(g + 1) * N]
        c_g = bc[:, SSD_GROUPS * N + g * N:SSD_GROUPS * N + (g + 1) * N].astype(BF16)
        b_t = b_g.T.astype(BF16)
        cb = jnp.dot(c_g, b_t, preferred_element_type=F32)
        ssq = jnp.zeros((L, 1), F32)
        for i in range(pairs_per_group):
            pi = g * pairs_per_group + i
            blk = slice(pi * LANES, (pi + 1) * LANES)
            x_blk = xdt[:, blk]
            y_pair = jnp.zeros((L, LANES), F32)
            for e in range(2):
                h = 2 * pi + e
                seg = jnp.exp(jnp.where(tril, a_cs[:, h:h + 1] - a_cs_t[h:h + 1, :], -jnp.inf))
                mat = (cb * seg).astype(BF16)
                keep = first_head if e == 0 else jnp.logical_not(first_head)
                y_pair = y_pair + jnp.dot(mat, jnp.where(keep, x_blk, 0.0).astype(BF16),
                                          preferred_element_type=F32)
            st = jnp.dot(b_t, xdec_b[:, blk], preferred_element_type=F32)
            h_prev = state_ref[pi]
            y_off = jnp.dot(c_g, h_prev.astype(BF16), preferred_element_type=F32) * exp_w[:, blk]
            state_ref[pi] = h_prev * exp_w[L - 1:L, blk] + st
            y = y_pair + y_off + xs[:, blk] * dskip_ref[:, blk]
            yg = y * _silu(z_ref[:, blk])
            ssq = ssq + jnp.sum(yg * yg, axis=-1, keepdims=True)
            y_ref[:, blk] = yg
        rs = lax.rsqrt(ssq / gw + RMS_EPS)
        gsl = slice(g * gw, (g + 1) * gw)
        o_ref[:, gsl] = (y_ref[:, gsl] * rs * ng_ref[:, gsl]).astype(o_ref.dtype)


def _pad_lanes(v, width=LANES):
    return jnp.pad(v.astype(F32), (0, width - v.shape[0])).reshape(1, width)


def _ssd(zx, hb, w_in_t, layer, conv_w, conv_b, dt_bias, a_log, d_skip, norm_g):
    s = zx.shape[0]
    d = hb.shape[1]
    dt_row = 3 * ATTN_WIDTH + ZX_WIDTH
    assert dt_row % SSD_HEADS == 0 and w_in_t.shape[1] == dt_row + SSD_HEADS
    L = SSD_CHUNK
    assert s % L == 0
    w = SSD_WIDTH
    expand = (jnp.arange(LANES)[:, None] == (jnp.arange(w)[None, :] // SSD_HEAD_DIM)).astype(BF16)
    d_wide = jnp.repeat(d_skip.astype(F32), SSD_HEAD_DIM).reshape(1, w)
    row1 = lambda width: pl.BlockSpec((1, width), lambda c: (0, 0))
    return pl.pallas_call(
        _ssd_kernel,
        out_shape=jax.ShapeDtypeStruct((s, w), BF16),
        grid=(s // L,),
        in_specs=[pl.BlockSpec((L, w), lambda c: (c, 0)),
                  pl.BlockSpec((L, w), lambda c: (c, 1)),
                  pl.BlockSpec((L, SSD_BC), lambda c: (c, 2 * w // SSD_BC)),
                  pl.BlockSpec((L, d), lambda c: (c, 0)),
                  pl.BlockSpec((None, SSD_HEADS, d), lambda c: (layer, dt_row // SSD_HEADS, 0)),
                  pl.BlockSpec((SSD_CONV, w), lambda c: (0, 0)),
                  pl.BlockSpec((SSD_CONV, SSD_BC), lambda c: (0, 0)),
                  row1(w), row1(SSD_BC), row1(LANES), row1(LANES), row1(w), row1(w),
                  pl.BlockSpec((LANES, w), lambda c: (0, 0))],
        out_specs=pl.BlockSpec((L, w), lambda c: (c, 0)),
        scratch_shapes=[pltpu.VMEM((L + SUBLANES, w), F32),
                        pltpu.VMEM((L + SUBLANES, SSD_BC), F32),
                        pltpu.VMEM((w // LANES, SSD_STATE, LANES), F32),
                        pltpu.VMEM((L, w), F32)],
        compiler_params=_params("arbitrary"),
        name="ssd_scan",
    )(zx, zx, zx, hb, w_in_t, conv_w[:, :w], conv_w[:, w:], conv_b[:w].reshape(1, w), conv_b[w:].reshape(1, SSD_BC),
      _pad_lanes(dt_bias), _pad_lanes(a_log), d_wide, norm_g.reshape(1, w), expand)


def _layer_norm(y, g, b):
    mu = jnp.mean(y, axis=-1, keepdims=True)
    d = y - mu
    var = jnp.mean(d * d, axis=-1, keepdims=True)
    return d * lax.rsqrt(var + LN_EPS) * g + b


def _outproj_kernel(attn_ref, ssd_ref, h_ref, w_ref, ag_ref, g_ref, b_ref, o_ref, ob_ref):
    attn = attn_ref[...]
    attn_n = attn * lax.rsqrt(jnp.mean(attn * attn, axis=-1, keepdims=True) + RMS_EPS) * ag_ref[...]
    aw = ATTN_WIDTH
    mix = jnp.dot(attn_n.astype(BF16), w_ref[0:aw, :], preferred_element_type=F32)
    mix = mix + jnp.dot(ssd_ref[...], w_ref[aw:, :], preferred_element_type=F32)
    out = _layer_norm(DEEPNORM_ALPHA * h_ref[...] + mix, g_ref[...], b_ref[...])
    o_ref[...] = out
    ob_ref[...] = out.astype(BF16)


def _outproj_ln(attn, ssd, h, w, layer, attn_g, ln_g, ln_b):
    s, d = h.shape
    tm = _tile(s, 512)
    row = lambda width: pl.BlockSpec((1, width), lambda i: (0, 0))
    return pl.pallas_call(
        _outproj_kernel,
        out_shape=(jax.ShapeDtypeStruct((s, d), F32), jax.ShapeDtypeStruct((s, d), BF16)),
        grid=(s // tm,),
        in_specs=[pl.BlockSpec((tm, ATTN_WIDTH), lambda i: (i, 0)),
                  pl.BlockSpec((tm, SSD_WIDTH), lambda i: (i, 0)),
                  pl.BlockSpec((tm, d), lambda i: (i, 0)),
                  pl.BlockSpec((None, d, d), lambda i: (layer, 0, 0), pipeline_mode=pl.Buffered(1)),
                  row(ATTN_WIDTH), row(d), row(d)],
        out_specs=(pl.BlockSpec((tm, d), lambda i: (i, 0)), pl.BlockSpec((tm, d), lambda i: (i, 0))),
        compiler_params=_params("parallel"),
        name="outproj_ln",
    )(attn, ssd, h, w, attn_g.reshape(1, -1), ln_g.reshape(1, -1), ln_b.reshape(1, -1))


def _ffn_up_kernel(x_ref, wg_ref, wu_ref, cwg_ref, cwu_ref, cbg_ref, cbu_ref, o_ref, gpad_ref, upad_ref,
                   wgb_ref, wub_ref):
    tm = x_ref.shape[0]
    halo = SUBLANES
    _cast_weights_once(wg_ref, wgb_ref)
    _cast_weights_once(wu_ref, wub_ref)

    @pl.when(pl.program_id(1) == 0)
    def _():
        gpad_ref[0:halo, :] = jnp.zeros((halo, gpad_ref.shape[1]), F32)
        upad_ref[0:halo, :] = jnp.zeros((halo, upad_ref.shape[1]), F32)

    x = x_ref[...]
    gpad_ref[halo:halo + tm, :] = jnp.dot(x, wgb_ref[...], preferred_element_type=F32)
    upad_ref[halo:halo + tm, :] = jnp.dot(x, wub_ref[...], preferred_element_type=F32)

    def conv(pad_ref, w_ref, b_ref):
        acc = b_ref[...]
        for k in range(FFN_CONV):
            off = halo - (FFN_CONV - 1) + k
            acc = acc + pad_ref[off:off + tm, :] * w_ref[k:k + 1, :]
        return acc

    g = conv(gpad_ref, cwg_ref, cbg_ref)
    u = conv(upad_ref, cwu_ref, cbu_ref)
    o_ref[...] = (_silu(g) * u).astype(o_ref.dtype)
    gpad_ref[0:halo, :] = gpad_ref[tm:tm + halo, :]
    upad_ref[0:halo, :] = upad_ref[tm:tm + halo, :]


def _ffn_up(hb, w_up, layer, conv_w, conv_b):
    s, d = hb.shape
    tm, tn = _tile(s, 1024), 512
    nj = D_FF // tn
    conv_b = conv_b.reshape(1, -1)
    return pl.pallas_call(
        _ffn_up_kernel,
        out_shape=jax.ShapeDtypeStruct((s, D_FF), BF16),
        grid=(nj, s // tm),
        in_specs=[pl.BlockSpec((tm, d), lambda j, i: (i, 0)),
                  pl.BlockSpec((None, d, tn), lambda j, i: (layer, 0, j)),
                  pl.BlockSpec((None, d, tn), lambda j, i: (layer, 0, j + nj)),
                  pl.BlockSpec((FFN_CONV, tn), lambda j, i: (0, j)),
                  pl.BlockSpec((FFN_CONV, tn), lambda j, i: (0, j + nj)),
                  pl.BlockSpec((1, tn), lambda j, i: (0, j)),
                  pl.BlockSpec((1, tn), lambda j, i: (0, j + nj))],
        out_specs=pl.BlockSpec((tm, tn), lambda j, i: (i, j)),
        scratch_shapes=[pltpu.VMEM((tm + SUBLANES, tn), F32), pltpu.VMEM((tm + SUBLANES, tn), F32),
                        pltpu.VMEM((d, tn), BF16), pltpu.VMEM((d, tn), BF16)],
        compiler_params=_params("parallel", "arbitrary"),
        name="ffn_up",
    )(hb, w_up, w_up, conv_w, conv_w, conv_b, conv_b)


def _ple_kernel(p_ref, hb_ref, wp_ref, wg_ref, o_ref, wpb_ref, wgb_ref):
    _cast_weights_once(wp_ref, wpb_ref)
    _cast_weights_once(wg_ref, wgb_ref)
    proj = jnp.dot(p_ref[...].astype(BF16), wpb_ref[...], preferred_element_type=F32)
    gate = jnp.dot(hb_ref[...], wgb_ref[...], preferred_element_type=F32)
    o_ref[...] = proj * jax.nn.sigmoid(gate)


def _ple(p, hb, w_proj, w_gate, layer):
    s, d = hb.shape
    tm, tn = _tile(s, 1024), 1024
    return pl.pallas_call(
        _ple_kernel,
        out_shape=jax.ShapeDtypeStruct((s, d), F32),
        grid=(d // tn, s // tm),
        in_specs=[pl.BlockSpec((None, None, tm, PE_DIM), lambda j, i: (layer, 0, i, 0)),
                  pl.BlockSpec((tm, d), lambda j, i: (i, 0)),
                  pl.BlockSpec((None, PE_DIM, tn), lambda j, i: (layer, 0, j)),
                  pl.BlockSpec((None, d, tn), lambda j, i: (layer, 0, j))],
        out_specs=pl.BlockSpec((tm, tn), lambda j, i: (i, j)),
        scratch_shapes=[pltpu.VMEM((PE_DIM, tn), BF16), pltpu.VMEM((d, tn), BF16)],
        compiler_params=_params("parallel", "arbitrary"),
        name="ple_gate",
    )(p, hb, w_proj, w_gate)


def _down_kernel(a_ref, w_ref, h_ref, hb_ref, p_ref, wg_ref, wp_ref, g_ref, b_ref, o_ref, ob_ref):
    ffn = jnp.dot(a_ref[...], w_ref[...], preferred_element_type=F32)
    gate = jnp.dot(hb_ref[...], wg_ref[...], preferred_element_type=F32)
    proj = jnp.dot(p_ref[...].astype(BF16), wp_ref[...], preferred_element_type=F32)
    ple = proj * jax.nn.sigmoid(gate)
    out = _layer_norm(DEEPNORM_ALPHA * h_ref[...] + ffn + ple, g_ref[...], b_ref[...])
    o_ref[...] = out
    ob_ref[...] = out.astype(BF16)


def _down_ln(act, w_down, w_gate, w_proj, layer, h, hb, p, ln_g, ln_b):
    s, d = h.shape
    tm = _tile(s, 256)
    row = lambda width: pl.BlockSpec((1, width), lambda i: (0, 0))
    resident = lambda rows: pl.BlockSpec((None, rows, d), lambda i: (layer, 0, 0), pipeline_mode=pl.Buffered(1))
    return pl.pallas_call(
        _down_kernel,
        out_shape=(jax.ShapeDtypeStruct((s, d), F32), jax.ShapeDtypeStruct((s, d), BF16)),
        grid=(s // tm,),
        in_specs=[pl.BlockSpec((tm, D_FF), lambda i: (i, 0)),
                  resident(D_FF),
                  pl.BlockSpec((tm, d), lambda i: (i, 0)),
                  pl.BlockSpec((tm, d), lambda i: (i, 0)),
                  pl.BlockSpec((None, None, tm, PE_DIM), lambda i: (layer, 0, i, 0)),
                  resident(d), resident(PE_DIM),
                  row(d), row(d)],
        out_specs=(pl.BlockSpec((tm, d), lambda i: (i, 0)), pl.BlockSpec((tm, d), lambda i: (i, 0))),
        compiler_params=_params("parallel"),
        name="down_ln",
    )(act, w_down, h, hb, p, w_gate, w_proj, ln_g.reshape(1, -1), ln_b.reshape(1, -1))


def kernel(x, p, positions, w_in, attn_norm_g, ssd_conv_w, ssd_conv_b, ssd_dt_bias, ssd_a_log, ssd_d, ssd_norm_g,
           w_out, ln1_g, ln1_b, w_up, ffn_conv_w, ffn_conv_b, w_down, w_pe_gate, w_pe_proj, ln2_g, ln2_b):
    b, s, d = x.shape
    assert b == 1 and d == D_MODEL
    cos, sin = _rope_tables(positions[0])
    h = x[0]
    hb = h.astype(BF16)
    w_in_t = jnp.swapaxes(w_in, 1, 2)
    w_out_b = w_out.astype(BF16)
    w_down_b = w_down.astype(BF16)
    w_gate_b = w_pe_gate.astype(BF16)
    w_proj_b = w_pe_proj.astype(BF16)
    for i in range(w_in.shape[0]):
        qkv = _qkv_proj(hb, w_in_t, i, cos, sin)
        zx = _zx_proj(hb, w_in_t, i)
        attn = _moba_attention(qkv)
        ssd = _ssd(zx, hb, w_in_t, i, ssd_conv_w[i], ssd_conv_b[i], ssd_dt_bias[i], ssd_a_log[i],
                   ssd_d[i], ssd_norm_g[i])
        h1, h1b = _outproj_ln(attn, ssd, h, w_out_b, i, attn_norm_g[i], ln1_g[i], ln1_b[i])
        act = _ffn_up(h1b, w_up, i, ffn_conv_w[i], ffn_conv_b[i])
        h, hb = _down_ln(act, w_down_b, w_gate_b, w_proj_b, i, h1, h1b, p, ln2_g[i], ln2_b[i])
    return h[None]
```
